```python
import jax, jax.numpy as jnp
from jax import lax
import numpy as np

D_MODEL = 1024
BATCH = 2
SEQ = 16384
DEPTH = 2

D_MIX = D_MODEL
N_GROUPS = 4
D_GROUP = D_MIX // N_GROUPS
HEAD_DIM = 64
N_HEADS_GROUP = D_GROUP // HEAD_DIM
CHUNK = 128
Q_BLOCK = 128
SHORT_CONV_W = 3
CONF_CONV_W = 31
D_FF = 4 * D_MODEL
EPS = 1e-6
D_IN_PROJ = 10 * D_GROUP
SPLIT_POINTS = (2 * D_GROUP, 5 * D_GROUP, 8 * D_GROUP)

kernel_name = 'hymba_parallel_gmlp_shortconv_stickbreak_conformer'


def rms_norm(x, g=None):
    xf = x.astype(jnp.float32)
    y = xf * lax.rsqrt(jnp.mean(xf * xf, axis=-1, keepdims=True) + EPS)
    if g is not None:
        y = y * g.astype(jnp.float32)
    return y.astype(x.dtype)


def layer_norm(x, g, b):
    xf = x.astype(jnp.float32)
    xc = xf - jnp.mean(xf, axis=-1, keepdims=True)
    var = jnp.mean(xc * xc, axis=-1, keepdims=True)
    y = xc * lax.rsqrt(var + EPS) * g.astype(jnp.float32) + b.astype(jnp.float32)
    return y.astype(x.dtype)


def causal_depthwise_conv(x, w):
    k_w, c = w.shape
    return lax.conv_general_dilated(
        x, w[:, None, :].astype(x.dtype), window_strides=(1,), padding=[(k_w - 1, 0)],
        dimension_numbers=('NWC', 'WIO', 'NWC'), feature_group_count=c)


def spatial_gating_mixer(z, v_gain, w_s, b_s):
    z = jax.nn.gelu(z)
    u, v = jnp.split(z, 2, axis=-1)
    v = rms_norm(v, v_gain)
    b, s, _ = v.shape
    v = v.reshape(b, s // CHUNK, CHUNK, N_HEADS_GROUP, HEAD_DIM)
    mask = jnp.tril(jnp.ones((CHUNK, CHUNK), dtype=bool))
    w = jnp.where(mask, w_s, jnp.zeros_like(w_s))
    f = jnp.einsum('hts,bnshd->bnthd', w, v) + b_s.T[:, :, None]
    return u * f.reshape(b, s, D_GROUP)


def short_conv_mixer(z, w_conv):
    gate_b, gate_c, h = jnp.split(z, 3, axis=-1)
    return gate_b * causal_depthwise_conv(gate_c * h, w_conv)


def stick_breaking_attention(z):
    q, k, v = jnp.split(z, 3, axis=-1)
    b, s, _ = q.shape
    q = q.reshape(b, s, N_HEADS_GROUP, HEAD_DIM)
    k = k.reshape(b, s, N_HEADS_GROUP, HEAD_DIM)
    v = v.reshape(b, s, N_HEADS_GROUP, HEAD_DIM)
    scale = HEAD_DIM ** -0.5
    n_blocks = s // Q_BLOCK
    q_blocks = q.reshape(b, n_blocks, Q_BLOCK, N_HEADS_GROUP, HEAD_DIM).transpose(1, 0, 2, 3, 4)
    key_pos = jnp.arange(s)

    def one_block(args):
        q_blk, blk_idx = args
        logits = jnp.einsum('bqhd,bkhd->bhqk', q_blk, k,
                            preferred_element_type=jnp.float32) * scale
        q_pos = blk_idx * Q_BLOCK + jnp.arange(Q_BLOCK)
        causal = key_pos[None, :] < q_pos[:, None]
        log_beta = jax.nn.log_sigmoid(logits)
        log_one_minus = jnp.where(causal, jax.nn.log_sigmoid(-logits), 0.0)
        log_stick = lax.cumsum(log_one_minus, axis=3, reverse=True) - log_one_minus
        weights = jnp.where(causal, jnp.exp(log_beta + log_stick), 0.0)
        return jnp.einsum('bhqk,bkhd->bqhd', weights.astype(v.dtype), v)

    out = lax.map(one_block, (q_blocks, jnp.arange(n_blocks)))
    return out.transpose(1, 0, 2, 3, 4).reshape(b, s, D_GROUP)


def conformer_conv_mixer(z, w_conv, ln_g, ln_b):
    a, g = jnp.split(z, 2, axis=-1)
    h = a * jax.nn.sigmoid(g)
    h = causal_depthwise_conv(h, w_conv)
    h = layer_norm(h, ln_g, ln_b)
    return jax.nn.silu(h)


def setup_inputs(seed: int = 0) -> dict:
    key = jax.random.key(seed)
    ks = jax.random.split(key, 17)
    nrm = jax.random.normal
    x = nrm(ks[0], (BATCH, SEQ, D_MODEL), jnp.float32)
    norm_mix_g = 1.0 + 0.05 * nrm(ks[1], (DEPTH, D_MODEL), jnp.float32)
    w_in = nrm(ks[2], (DEPTH, D_MODEL, D_IN_PROJ), jnp.float32) * D_MODEL ** -0.5
    gmlp_v_g = 1.0 + 0.05 * nrm(ks[3], (DEPTH, D_GROUP), jnp.float32)
    gmlp_w_s = nrm(ks[4], (DEPTH, N_HEADS_GROUP, CHUNK, CHUNK), jnp.float32) * CHUNK ** -0.5
    gmlp_b_s = 1.0 + 0.05 * nrm(ks[5], (DEPTH, N_HEADS_GROUP, CHUNK), jnp.float32)
    short_conv_w = nrm(ks[6], (DEPTH, SHORT_CONV_W, D_GROUP), jnp.float32) * SHORT_CONV_W ** -0.5
    conf_conv_w = nrm(ks[7], (DEPTH, CONF_CONV_W, D_GROUP), jnp.float32) * CONF_CONV_W ** -0.5
    conf_ln_g = 1.0 + 0.05 * nrm(ks[8], (DEPTH, D_GROUP), jnp.float32)
    conf_ln_b = 0.02 * nrm(ks[9], (DEPTH, D_GROUP), jnp.float32)
    mix_out_g = 1.0 + 0.05 * nrm(ks[10], (DEPTH, D_MIX), jnp.float32)
    w_out = nrm(ks[11], (DEPTH, D_MIX, D_MODEL), jnp.float32) * D_MIX ** -0.5
    norm_ffn_g = 1.0 + 0.05 * nrm(ks[12], (DEPTH, D_MODEL), jnp.float32)
    w_up = nrm(ks[13], (DEPTH, D_MODEL, D_FF), jnp.float32) * D_MODEL ** -0.5
    w_down = nrm(ks[14], (DEPTH, D_FF, D_MODEL), jnp.float32) * D_FF ** -0.5
    final_norm_g = 1.0 + 0.05 * nrm(ks[15], (D_MODEL,), jnp.float32)
    return {'x': x, 'norm_mix_g': norm_mix_g, 'w_in': w_in, 'gmlp_v_g': gmlp_v_g,
            'gmlp_w_s': gmlp_w_s, 'gmlp_b_s': gmlp_b_s, 'short_conv_w': short_conv_w,
            'conf_conv_w': conf_conv_w, 'conf_ln_g': conf_ln_g, 'conf_ln_b': conf_ln_b,
            'mix_out_g': mix_out_g, 'w_out': w_out, 'norm_ffn_g': norm_ffn_g,
            'w_up': w_up, 'w_down': w_down, 'final_norm_g': final_norm_g}


def reference(x, norm_mix_g, w_in, gmlp_v_g, gmlp_w_s, gmlp_b_s, short_conv_w, conf_conv_w,
              conf_ln_g, conf_ln_b, mix_out_g, w_out, norm_ffn_g, w_up, w_down, final_norm_g):
    for l in range(DEPTH):
        h = rms_norm(x, norm_mix_g[l])
        z = jnp.einsum('bsd,de->bse', h, w_in[l])
        z_a, z_b, z_c, z_d = jnp.split(z, SPLIT_POINTS, axis=-1)
        y_a = spatial_gating_mixer(z_a, gmlp_v_g[l], gmlp_w_s[l], gmlp_b_s[l])
        y_b = short_conv_mixer(z_b, short_conv_w[l])
        y_c = stick_breaking_attention(z_c)
        y_d = conformer_conv_mixer(z_d, conf_conv_w[l], conf_ln_g[l], conf_ln_b[l])
        y = jnp.concatenate([rms_norm(y_a), rms_norm(y_b), rms_norm(y_c), rms_norm(y_d)],
                            axis=-1) * mix_out_g[l]
        x = x + jnp.einsum('bse,ed->bsd', y, w_out[l])
        h = rms_norm(x, norm_ffn_g[l])
        a = jax.nn.relu(jnp.einsum('bsd,df->bsf', h, w_up[l]))
        x = x + jnp.einsum('bsf,fd->bsd', a * a, w_down[l])
    return rms_norm(x, final_norm_g)
```

```python
import functools
import math

import jax
import jax.numpy as jnp
from jax import lax
from jax.experimental import pallas as pl
from jax.experimental.pallas import tpu as pltpu

D_MODEL = 1024
D_GROUP = 256
HEAD_DIM = 64
N_HEADS = D_GROUP // HEAD_DIM
CHUNK = 128
SHORT_W = 3
CONF_W = 31
D_FF = 4 * D_MODEL
EPS = 1e-6

F32 = jnp.float32
BF16 = jnp.bfloat16

V7X_VMEM_BYTES = 64 * 1024 * 1024
LANES = 128
SUBLANES = 8

TM_PROJ = 512
TM_MIX = 256
QB = 128
SHORT_HALO = SUBLANES
CONF_HALO = 4 * SUBLANES
STICK_DEAD = 110.0


def _rms(x):
    return x * lax.rsqrt(jnp.mean(x * x, axis=-1, keepdims=True) + EPS)


def _in_proj_kernel(x_ref, g_ref, w_ref, za_ref, zb_ref, q_ref, k_ref, v_ref, zd_ref):
    h = (_rms(x_ref[...]) * g_ref[...]).astype(BF16)

    def proj(lo, hi):
        return jnp.dot(h, w_ref[:, lo:hi], preferred_element_type=F32)

    g = D_GROUP
    za_ref[...] = proj(0, 2 * g)
    zb_ref[...] = proj(2 * g, 5 * g)
    q_ref[...] = (proj(5 * g, 6 * g) * (HEAD_DIM ** -0.5)).astype(BF16)
    k_ref[...] = proj(6 * g, 7 * g).astype(BF16)
    v_ref[...] = proj(7 * g, 8 * g).astype(BF16)
    zd_ref[...] = proj(8 * g, 10 * g)


def _in_proj(x2, g, w_bf):
    n = x2.shape[0]
    tm = TM_PROJ
    row = lambda i: (i, 0)
    const = lambda i: (0, 0)
    outs = [
        jax.ShapeDtypeStruct((n, 2 * D_GROUP), F32),
        jax.ShapeDtypeStruct((n, 3 * D_GROUP), F32),
        jax.ShapeDtypeStruct((n, D_GROUP), BF16),
        jax.ShapeDtypeStruct((n, D_GROUP), BF16),
        jax.ShapeDtypeStruct((n, D_GROUP), BF16),
        jax.ShapeDtypeStruct((n, 2 * D_GROUP), F32),
    ]
    return pl.pallas_call(
        _in_proj_kernel,
        grid=(n // tm,),
        in_specs=[
            pl.BlockSpec((tm, D_MODEL), row),
            pl.BlockSpec((1, D_MODEL), const),
            pl.BlockSpec(w_bf.shape, const),
        ],
        out_specs=[pl.BlockSpec((tm, o.shape[1]), row) for o in outs],
        out_shape=outs,
        compiler_params=pltpu.CompilerParams(
            dimension_semantics=("arbitrary",), vmem_limit_bytes=40 * 1024 * 1024),
        name="in_proj",
    )(x2, g, w_bf)


def _gelu_tanh(x):
    c = math.sqrt(2.0 / math.pi)
    return 0.5 * x * (1.0 + jnp.tanh(c * (x + 0.044715 * (x * x * x))))


def _local_mix_kernel(za_ref, zb_ref, zd_ref, vg_ref, ws_ref, bs_ref, wsc_ref, wcc_ref,
                      lng_ref, lnb_ref, go_ref, y_ref, pbuf, hbuf):
    t = za_ref.shape[0]
    g = D_GROUP

    @pl.when(pl.program_id(1) == 0)
    def _():
        pbuf[0:SHORT_HALO, :] = jnp.zeros((SHORT_HALO, g), F32)
        hbuf[0:CONF_HALO, :] = jnp.zeros((CONF_HALO, g), F32)

    pbuf[SHORT_HALO:SHORT_HALO + t, :] = zb_ref[:, g:2 * g] * zb_ref[:, 2 * g:3 * g]
    hbuf[CONF_HALO:CONF_HALO + t, :] = zd_ref[:, 0:g] * jax.nn.sigmoid(zd_ref[:, g:2 * g])

    rr = lax.broadcasted_iota(jnp.int32, (CHUNK, N_HEADS * CHUNK), 0)
    cc = lax.broadcasted_iota(jnp.int32, (CHUNK, N_HEADS * CHUNK), 1)
    wcat = jnp.where((cc % CHUNK) <= rr, ws_ref[...], 0.0).astype(BF16)
    head_of_lane = lax.broadcasted_iota(jnp.int32, (CHUNK, g), 1) // HEAD_DIM

    for c in range(t // CHUNK):
        r0 = c * CHUNK
        ga = _gelu_tanh(za_ref[r0:r0 + CHUNK, :])
        u = ga[:, 0:g]
        v = (_rms(ga[:, g:2 * g]) * vg_ref[...]).astype(BF16)
        vbd = jnp.concatenate(
            [jnp.where(head_of_lane == h, v, jnp.zeros_like(v)) for h in range(N_HEADS)], axis=0)
        f = jnp.dot(wcat, vbd, preferred_element_type=F32) + bs_ref[...]
        ya = u * f
        conv_b = wsc_ref[SHORT_W - 1:SHORT_W, :] * pbuf[SHORT_HALO + r0:SHORT_HALO + r0 + CHUNK, :]
        for kk in range(SHORT_W - 1):
            off = SHORT_HALO + r0 - (SHORT_W - 1) + kk
            conv_b = conv_b + wsc_ref[kk:kk + 1, :] * pbuf[off:off + CHUNK, :]
        yb = zb_ref[r0:r0 + CHUNK, 0:g] * conv_b
        conv_d = None
        for kk in range(CONF_W):
            off = CONF_HALO + r0 - (CONF_W - 1) + kk
            term = wcc_ref[kk:kk + 1, :] * hbuf[off:off + CHUNK, :]
            conv_d = term if conv_d is None else conv_d + term
        xc = conv_d - jnp.mean(conv_d, axis=-1, keepdims=True)
        var = jnp.mean(xc * xc, axis=-1, keepdims=True)
        ln = xc * lax.rsqrt(var + EPS) * lng_ref[...] + lnb_ref[...]
        yd = ln * jax.nn.sigmoid(ln)
        y_ref[r0:r0 + CHUNK, 0:g] = (_rms(ya) * go_ref[:, 0:g]).astype(BF16)
        y_ref[r0:r0 + CHUNK, g:2 * g] = (_rms(yb) * go_ref[:, g:2 * g]).astype(BF16)
        y_ref[r0:r0 + CHUNK, 2 * g:3 * g] = (_rms(yd) * go_ref[:, 2 * g:3 * g]).astype(BF16)

    pbuf[0:SHORT_HALO, :] = pbuf[t:t + SHORT_HALO, :]
    hbuf[0:CONF_HALO, :] = hbuf[t:t + CONF_HALO, :]


def _local_mix(za, zb, zd, vg, ws_cat, bs_lanes, w_short, w_conf, ln_g, ln_b, g_abd):
    b, s, _ = za.shape
    t = TM_MIX
    tile = lambda bi, i: (bi, i, 0)
    const = lambda bi, i: (0, 0)
    g = D_GROUP
    return pl.pallas_call(
        _local_mix_kernel,
        grid=(b, s // t),
        in_specs=[
            pl.BlockSpec((None, t, 2 * g), tile),
            pl.BlockSpec((None, t, 3 * g), tile),
            pl.BlockSpec((None, t, 2 * g), tile),
            pl.BlockSpec((1, g), const),
            pl.BlockSpec((CHUNK, N_HEADS * CHUNK), const),
            pl.BlockSpec((CHUNK, g), const),
            pl.BlockSpec((SHORT_W, g), const),
            pl.BlockSpec((CONF_W, g), const),
            pl.BlockSpec((1, g), const),
            pl.BlockSpec((1, g), const),
            pl.BlockSpec((1, 3 * g), const),
        ],
        out_specs=pl.BlockSpec((None, t, 3 * g), tile),
        out_shape=jax.ShapeDtypeStruct((b, s, 3 * g), BF16),
        scratch_shapes=[
            pltpu.VMEM((SHORT_HALO + t, g), F32),
            pltpu.VMEM((CONF_HALO + t, g), F32),
        ],
        compiler_params=pltpu.CompilerParams(
            dimension_semantics=("arbitrary", "arbitrary"), vmem_limit_bytes=32 * 1024 * 1024),
        name="local_mix",
    )(za, zb, zd, vg, ws_cat, bs_lanes, w_short, w_conf, ln_g, ln_b, g_abd)


def _attn_kernel(q_ref, k_ref, v_ref, go_ref, y_ref, carry_ref, acc_ref):
    i = pl.program_id(1)
    g = D_GROUP
    nh = N_HEADS
    head_of_lane = lax.broadcasted_iota(jnp.int32, (QB, g), 1) // HEAD_DIM
    q = q_ref[...]
    qs = jnp.concatenate(
        [jnp.where(head_of_lane == h, q, jnp.zeros_like(q)) for h in range(nh)], axis=0)
    rr = lax.broadcasted_iota(jnp.int32, (QB, 2 * QB), 0)
    cc = lax.broadcasted_iota(jnp.int32, (QB, 2 * QB), 1)
    tt = jnp.where((cc >= QB) | (rr > cc), 1.0, 0.0).astype(BF16)
    qrow = lax.broadcasted_iota(jnp.int32, (nh * QB, QB), 0) % QB
    kcol = lax.broadcasted_iota(jnp.int32, (nh * QB, QB), 1)
    causal = kcol < qrow

    def block(kb, diag):
        start = pl.multiple_of(kb * QB, QB)
        kblk = k_ref[pl.ds(start, QB), :]
        vblk = v_ref[pl.ds(start, QB), :]
        s = lax.dot_general(qs, kblk, (((1,), (1,)), ((), ())), preferred_element_type=F32)
        sp = jnp.maximum(s, 0.0) + jnp.log1p(jnp.exp(-jnp.abs(s)))
        lom = jnp.where(causal, -sp, 0.0) if diag else -sp
        hi = lom.astype(BF16)
        lo = (lom - hi.astype(F32)).astype(BF16)
        ll = jnp.dot(hi, tt, preferred_element_type=F32) + jnp.dot(lo, tt, preferred_element_type=F32)
        logw = (s - sp) + ll[:, 0:QB]
        if not diag:
            logw = logw + carry_ref[...]
        w = jnp.exp(logw)
        if diag:
            w = jnp.where(causal, w, 0.0)
            carry_ref[...] = ll[:, QB:2 * QB]
        else:
            carry_ref[...] = carry_ref[...] + ll[:, QB:2 * QB]
        wb = w.astype(BF16)
        wcat = jnp.concatenate([wb[h * QB:(h + 1) * QB, :] for h in range(nh)], axis=1)
        vbd = jnp.concatenate(
            [jnp.where(head_of_lane == h, vblk, jnp.zeros_like(vblk)) for h in range(nh)], axis=0)
        pv = jnp.dot(wcat, vbd, preferred_element_type=F32)
        if diag:
            acc_ref[...] = pv
        else:
            acc_ref[...] = acc_ref[...] + pv

    block(i, True)

    def cond(st):
        kb, live = st
        return jnp.logical_and(kb >= 0, live > -STICK_DEAD)

    def body(st):
        kb, _ = st
        block(kb, False)
        return kb - 1, jnp.max(carry_ref[...])

    lax.while_loop(cond, body, (i - 1, jnp.max(carry_ref[...])))

    y_ref[...] = (_rms(acc_ref[...]) * go_ref[...]).astype(BF16)


def _attention(q, k, v, g_c):
    b, s, g = q.shape
    return pl.pallas_call(
        _attn_kernel,
        grid=(b, s // QB),
        in_specs=[
            pl.BlockSpec((None, QB, g), lambda bi, i: (bi, i, 0)),
            pl.BlockSpec((None, s, g), lambda bi, i: (bi, 0, 0)),
            pl.BlockSpec((None, s, g), lambda bi, i: (bi, 0, 0)),
            pl.BlockSpec((1, g), lambda bi, i: (0, 0)),
        ],
        out_specs=pl.BlockSpec((None, QB, g), lambda bi, i: (bi, i, 0)),
        out_shape=jax.ShapeDtypeStruct((b, s, g), BF16),
        scratch_shapes=[
            pltpu.VMEM((N_HEADS * QB, QB), F32),
            pltpu.VMEM((QB, g), F32),
        ],
        compiler_params=pltpu.CompilerParams(
            dimension_semantics=("arbitrary", "arbitrary"), vmem_limit_bytes=48 * 1024 * 1024),
        name="stick_attention",
    )(q, k, v, g_c)


def _out_ffn_kernel(x_ref, yabd_ref, yc_ref, wo_abd_ref, wo_c_ref, gf_ref, wup_ref, wdn_ref, gfin_ref,
                    o_ref, *, final_norm):
    x1 = (x_ref[...]
          + jnp.dot(yabd_ref[...], wo_abd_ref[...], preferred_element_type=F32)
          + jnp.dot(yc_ref[...], wo_c_ref[...], preferred_element_type=F32))
    h = (_rms(x1) * gf_ref[...]).astype(BF16)
    a = jnp.maximum(jnp.dot(h, wup_ref[...], preferred_element_type=F32), 0.0)
    x2 = x1 + jnp.dot((a * a).astype(BF16), wdn_ref[...], preferred_element_type=F32)
    if final_norm:
        x2 = _rms(x2) * gfin_ref[...]
    o_ref[...] = x2


def _out_ffn(x2, y_abd, y_c, wo_abd, wo_c, g_ffn, w_up, w_dn, g_fin, final_norm):
    n = x2.shape[0]
    tm = TM_PROJ
    row = lambda i: (i, 0)
    const = lambda i: (0, 0)
    resident = functools.partial(pl.BlockSpec, index_map=const, pipeline_mode=pl.Buffered(1))
    return pl.pallas_call(
        functools.partial(_out_ffn_kernel, final_norm=final_norm),
        grid=(n // tm,),
        in_specs=[
            pl.BlockSpec((tm, D_MODEL), row),
            pl.BlockSpec((tm, 3 * D_GROUP), row),
            pl.BlockSpec((tm, D_GROUP), row),
            resident(wo_abd.shape),
            resident(wo_c.shape),
            pl.BlockSpec((1, D_MODEL), const),
            resident(w_up.shape),
            resident(w_dn.shape),
            pl.BlockSpec((1, D_MODEL), const),
        ],
        out_specs=pl.BlockSpec((tm, D_MODEL), row),
        out_shape=jax.ShapeDtypeStruct((n, D_MODEL), F32),
        compiler_params=pltpu.CompilerParams(
            dimension_semantics=("arbitrary",), vmem_limit_bytes=56 * 1024 * 1024),
        name="out_ffn",
    )(x2, y_abd, y_c, wo_abd, wo_c, g_ffn, w_up, w_dn, g_fin)


def kernel(x, norm_mix_g, w_in, gmlp_v_g, gmlp_w_s, gmlp_b_s, short_conv_w, conf_conv_w, conf_ln_g, conf_ln_b,
           mix_out_g, w_out, norm_ffn_g, w_up, w_down, final_norm_g):
    b, s, d = x.shape
    depth = w_in.shape[0]
    g = D_GROUP
    n = b * s
    x2 = x.reshape(n, d)
    for l in range(depth):
        za, zb, q, k, v, zd = _in_proj(x2, norm_mix_g[l][None, :], w_in[l].astype(BF16))
        ws_cat = jnp.transpose(gmlp_w_s[l], (1, 0, 2)).reshape(CHUNK, N_HEADS * CHUNK)
        bs_lanes = jnp.repeat(gmlp_b_s[l].T, HEAD_DIM, axis=1)
        go = mix_out_g[l]
        g_abd = jnp.concatenate([go[0:2 * g], go[3 * g:4 * g]])[None, :]
        y_abd = _local_mix(
            za.reshape(b, s, 2 * g), zb.reshape(b, s, 3 * g), zd.reshape(b, s, 2 * g),
            gmlp_v_g[l][None, :], ws_cat, bs_lanes, short_conv_w[l], conf_conv_w[l],
            conf_ln_g[l][None, :], conf_ln_b[l][None, :], g_abd)
        y_c = _attention(q.reshape(b, s, g), k.reshape(b, s, g), v.reshape(b, s, g), go[2 * g:3 * g][None, :])
        wo = w_out[l].astype(BF16)
        wo_abd = jnp.concatenate([wo[0:2 * g], wo[3 * g:4 * g]], axis=0)
        x2 = _out_ffn(
            x2, y_abd.reshape(n, 3 * g), y_c.reshape(n, g), wo_abd, wo[2 * g:3 * g],
            norm_ffn_g[l][None, :], w_up[l].astype(BF16), w_down[l].astype(BF16),
            final_norm_g[None, :], final_norm=(l == depth - 1))
    return x2.reshape(b, s, d)
```

```python
import functools
import math

import jax
import jax.numpy as jnp
from jax import lax
from jax.experimental import pallas as pl
from jax.experimental.pallas import tpu as pltpu

D_MODEL = 1024
D_GROUP = 256
HEAD_DIM = 64
N_HEADS = D_GROUP // HEAD_DIM
CHUNK = 128
SHORT_W = 3
CONF_W = 31
D_FF = 4 * D_MODEL
EPS = 1e-6

F32 = jnp.float32
BF16 = jnp.bfloat16

V7X_VMEM_BYTES = 64 * 1024 * 1024
LANES = 128
SUBLANES = 8

TM_PROJ = 512
TM_MIX = 256
QB = 128
ATTN_WINDOW = 3
SHORT_HALO = SUBLANES
CONF_HALO = 4 * SUBLANES
STICK_DEAD = 110.0


def _rms(x):
    return x * lax.rsqrt(jnp.mean(x * x, axis=-1, keepdims=True) + EPS)


def _in_proj_kernel(x_ref, g_ref, w_ref, za_ref, zb_ref, q_ref, k_ref, v_ref, zd_ref):
    h = (_rms(x_ref[...]) * g_ref[...]).astype(BF16)

    def proj(lo, hi):
        return jnp.dot(h, w_ref[:, lo:hi], preferred_element_type=F32)

    g = D_GROUP
    za_ref[...] = proj(0, 2 * g)
    zb_ref[...] = proj(2 * g, 5 * g)
    q_ref[...] = (proj(5 * g, 6 * g) * -(HEAD_DIM ** -0.5)).astype(BF16)
    k_ref[...] = proj(6 * g, 7 * g).astype(BF16)
    v_ref[...] = proj(7 * g, 8 * g).astype(BF16)
    zd_ref[...] = proj(8 * g, 10 * g)


def _in_proj(x2, g, w_bf):
    n = x2.shape[0]
    tm = TM_PROJ
    row = lambda i: (i, 0)
    const = lambda i: (0, 0)
    outs = [
        jax.ShapeDtypeStruct((n, 2 * D_GROUP), F32),
        jax.ShapeDtypeStruct((n, 3 * D_GROUP), F32),
        jax.ShapeDtypeStruct((n, D_GROUP), BF16),
        jax.ShapeDtypeStruct((n, D_GROUP), BF16),
        jax.ShapeDtypeStruct((n, D_GROUP), BF16),
        jax.ShapeDtypeStruct((n, 2 * D_GROUP), F32),
    ]
    return pl.pallas_call(
        _in_proj_kernel,
        grid=(n // tm,),
        in_specs=[
            pl.BlockSpec((tm, D_MODEL), row),
            pl.BlockSpec((1, D_MODEL), const),
            pl.BlockSpec(w_bf.shape, const),
        ],
        out_specs=[pl.BlockSpec((tm, o.shape[1]), row) for o in outs],
        out_shape=outs,
        compiler_params=pltpu.CompilerParams(
            dimension_semantics=("arbitrary",), vmem_limit_bytes=40 * 1024 * 1024),
        name="in_proj",
    )(x2, g, w_bf)


def _gelu_tanh(x):
    c = math.sqrt(2.0 / math.pi)
    return 0.5 * x * (1.0 + jnp.tanh(c * (x + 0.044715 * (x * x * x))))


def _local_mix_kernel(za_ref, zb_ref, zd_ref, vg_ref, ws_ref, bs_ref, wsc_ref, wcc_ref,
                      lng_ref, lnb_ref, go_ref, y_ref, pbuf, hbuf):
    t = za_ref.shape[0]
    g = D_GROUP

    @pl.when(pl.program_id(1) == 0)
    def _():
        pbuf[0:SHORT_HALO, :] = jnp.zeros((SHORT_HALO, g), F32)
        hbuf[0:CONF_HALO, :] = jnp.zeros((CONF_HALO, g), F32)
        hbuf[CONF_HALO + t:CONF_HALO + t + SUBLANES, :] = jnp.zeros((SUBLANES, g), F32)

    pbuf[SHORT_HALO:SHORT_HALO + t, :] = zb_ref[:, g:2 * g] * zb_ref[:, 2 * g:3 * g]
    hbuf[CONF_HALO:CONF_HALO + t, :] = zd_ref[:, 0:g] * jax.nn.sigmoid(zd_ref[:, g:2 * g])

    rr = lax.broadcasted_iota(jnp.int32, (CHUNK, N_HEADS * CHUNK), 0)
    cc = lax.broadcasted_iota(jnp.int32, (CHUNK, N_HEADS * CHUNK), 1)
    wcat = jnp.where((cc % CHUNK) <= rr, ws_ref[...], 0.0).astype(BF16)
    head_of_lane = lax.broadcasted_iota(jnp.int32, (CHUNK, g), 1) // HEAD_DIM

    for c in range(t // CHUNK):
        r0 = c * CHUNK
        ga = _gelu_tanh(za_ref[r0:r0 + CHUNK, :])
        u = ga[:, 0:g]
        v = (_rms(ga[:, g:2 * g]) * vg_ref[...]).astype(BF16)
        vbd = jnp.concatenate(
            [jnp.where(head_of_lane == h, v, jnp.zeros_like(v)) for h in range(N_HEADS)], axis=0)
        f = jnp.dot(wcat, vbd, preferred_element_type=F32) + bs_ref[...]
        ya = u * f
        conv_b = wsc_ref[SHORT_W - 1:SHORT_W, :] * pbuf[SHORT_HALO + r0:SHORT_HALO + r0 + CHUNK, :]
        for kk in range(SHORT_W - 1):
            off = SHORT_HALO + r0 - (SHORT_W - 1) + kk
            conv_b = conv_b + wsc_ref[kk:kk + 1, :] * pbuf[off:off + CHUNK, :]
        yb = zb_ref[r0:r0 + CHUNK, 0:g] * conv_b
        base = CONF_HALO + r0 - (CONF_W - 1)
        conv_d = None
        for res in range(SUBLANES):
            part = None
            for kk in range(CONF_W):
                if (base + kk) % SUBLANES != res:
                    continue
                off = base + kk - res
                term = wcc_ref[kk:kk + 1, :] * hbuf[off:off + CHUNK + SUBLANES, :]
                part = term if part is None else part + term
            if part is None:
                continue
            shifted = part[res:res + CHUNK, :]
            conv_d = shifted if conv_d is None else conv_d + shifted
        xc = conv_d - jnp.mean(conv_d, axis=-1, keepdims=True)
        var = jnp.mean(xc * xc, axis=-1, keepdims=True)
        ln = xc * lax.rsqrt(var + EPS) * lng_ref[...] + lnb_ref[...]
        yd = ln * jax.nn.sigmoid(ln)
        y_ref[r0:r0 + CHUNK, 0:g] = (_rms(ya) * go_ref[:, 0:g]).astype(BF16)
        y_ref[r0:r0 + CHUNK, g:2 * g] = (_rms(yb) * go_ref[:, g:2 * g]).astype(BF16)
        y_ref[r0:r0 + CHUNK, 2 * g:3 * g] = (_rms(yd) * go_ref[:, 2 * g:3 * g]).astype(BF16)

    pbuf[0:SHORT_HALO, :] = pbuf[t:t + SHORT_HALO, :]
    hbuf[0:CONF_HALO, :] = hbuf[t:t + CONF_HALO, :]


def _local_mix(za, zb, zd, vg, ws_cat, bs_lanes, w_short, w_conf, ln_g, ln_b, g_abd):
    b, s, _ = za.shape
    t = TM_MIX
    tile = lambda bi, i: (bi, i, 0)
    const = lambda bi, i: (0, 0)
    g = D_GROUP
    return pl.pallas_call(
        _local_mix_kernel,
        grid=(b, s // t),
        in_specs=[
            pl.BlockSpec((None, t, 2 * g), tile),
            pl.BlockSpec((None, t, 3 * g), tile),
            pl.BlockSpec((None, t, 2 * g), tile),
            pl.BlockSpec((1, g), const),
            pl.BlockSpec((CHUNK, N_HEADS * CHUNK), const),
            pl.BlockSpec((CHUNK, g), const),
            pl.BlockSpec((SHORT_W, g), const),
            pl.BlockSpec((CONF_W, g), const),
            pl.BlockSpec((1, g), const),
            pl.BlockSpec((1, g), const),
            pl.BlockSpec((1, 3 * g), const),
        ],
        out_specs=pl.BlockSpec((None, t, 3 * g), tile),
        out_shape=jax.ShapeDtypeStruct((b, s, 3 * g), BF16),
        scratch_shapes=[
            pltpu.VMEM((SHORT_HALO + t, g), F32),
            pltpu.VMEM((CONF_HALO + t + SUBLANES, g), F32),
        ],
        compiler_params=pltpu.CompilerParams(
            dimension_semantics=("arbitrary", "arbitrary"), vmem_limit_bytes=32 * 1024 * 1024),
        name="local_mix",
    )(za, zb, zd, vg, ws_cat, bs_lanes, w_short, w_conf, ln_g, ln_b, g_abd)


def _attn_kernel(q_ref, k_ref, v_ref, go_ref, y_ref, carry_ref, acc_ref):
    i = pl.program_id(1)
    g = D_GROUP
    nh = N_HEADS
    head_of_lane = lax.broadcasted_iota(jnp.int32, (QB, g), 1) // HEAD_DIM
    q = q_ref[...]
    qs = jnp.concatenate(
        [jnp.where(head_of_lane == h, q, jnp.zeros_like(q)) for h in range(nh)], axis=0)
    rr = lax.broadcasted_iota(jnp.int32, (2 * QB, 2 * QB), 0) % QB
    cc = lax.broadcasted_iota(jnp.int32, (2 * QB, 2 * QB), 1)
    tt = jnp.where((cc >= QB) | (rr > cc), 1.0, 0.0).astype(BF16)
    qrow = lax.broadcasted_iota(jnp.int32, (nh * QB, QB), 0) % QB
    kcol = lax.broadcasted_iota(jnp.int32, (nh * QB, QB), 1)
    causal = kcol < qrow

    def neg_softplus(ns):
        return jnp.minimum(ns, 0.0) - jnp.log(1.0 + jnp.exp(-jnp.abs(ns)))

    def split(lom):
        hi = lom.astype(BF16)
        lo = (lom - hi.astype(F32)).astype(BF16)
        return jnp.concatenate([hi, lo], axis=1)

    def heads_to_lanes(wb):
        return [wb[h * QB:(h + 1) * QB, :] for h in range(nh)]

    def block_diag(vblk):
        return [jnp.where(head_of_lane == h, vblk, jnp.zeros_like(vblk)) for h in range(nh)]

    def load(ref, kb):
        return ref[pl.ds(pl.multiple_of(kb * QB, QB), QB), :]

    nw = ATTN_WINDOW
    kbs = [i - (nw - 1) + j for j in range(nw)]
    exists = [kb >= 0 for kb in kbs]
    kidx = [jnp.maximum(kb, 0) for kb in kbs]
    kwin = jnp.concatenate([load(k_ref, kb) for kb in kidx], axis=0)
    ns = lax.dot_general(qs, kwin, (((1,), (1,)), ((), ())), preferred_element_type=F32)
    ns_j = [ns[:, j * QB:(j + 1) * QB] for j in range(nw)]
    lom_j = []
    for j in range(nw):
        lom = neg_softplus(ns_j[j])
        lom_j.append(jnp.where(causal, lom, 0.0) if j == nw - 1 else jnp.where(exists[j], lom, 0.0))
    ll = jnp.dot(jnp.concatenate([split(l) for l in lom_j], axis=0), tt, preferred_element_type=F32)
    later = None
    wcols = [None] * nw
    for j in reversed(range(nw)):
        rows = slice(j * nh * QB, (j + 1) * nh * QB)
        logw = (lom_j[j] - ns_j[j]) + ll[rows, 0:QB]
        if later is not None:
            logw = logw + later
        w = jnp.exp(logw)
        w = jnp.where(causal, w, 0.0) if j == nw - 1 else jnp.where(exists[j], w, 0.0)
        wcols[j] = heads_to_lanes(w.astype(BF16))
        tot = ll[rows, QB:2 * QB]
        later = tot if later is None else later + tot
    carry_ref[...] = later
    wcat = jnp.concatenate([c for cols in wcols for c in cols], axis=1)
    vbd = jnp.concatenate([m for kb in kidx for m in block_diag(load(v_ref, kb))], axis=0)
    acc_ref[...] = jnp.dot(wcat, vbd, preferred_element_type=F32)

    def block(kb):
        nsb = lax.dot_general(qs, load(k_ref, kb), (((1,), (1,)), ((), ())), preferred_element_type=F32)
        lom = neg_softplus(nsb)
        llb = jnp.dot(split(lom), tt, preferred_element_type=F32)
        carry = carry_ref[...]
        w = jnp.exp((lom - nsb) + llb[:, 0:QB] + carry)
        carry_ref[...] = carry + llb[:, QB:2 * QB]
        wcat_b = jnp.concatenate(heads_to_lanes(w.astype(BF16)), axis=1)
        vbd_b = jnp.concatenate(block_diag(load(v_ref, kb)), axis=0)
        acc_ref[...] = acc_ref[...] + jnp.dot(wcat_b, vbd_b, preferred_element_type=F32)

    def cond(st):
        kb, live = st
        return jnp.logical_and(kb >= 0, live > -STICK_DEAD)

    def body(st):
        kb, _ = st
        block(kb)
        return kb - 1, jnp.max(carry_ref[...])

    lax.while_loop(cond, body, (i - nw, jnp.max(carry_ref[...])))

    y_ref[...] = (_rms(acc_ref[...]) * go_ref[...]).astype(BF16)


def _attention(q, k, v, g_c):
    b, s, g = q.shape
    return pl.pallas_call(
        _attn_kernel,
        grid=(b, s // QB),
        in_specs=[
            pl.BlockSpec((None, QB, g), lambda bi, i: (bi, i, 0)),
            pl.BlockSpec((None, s, g), lambda bi, i: (bi, 0, 0)),
            pl.BlockSpec((None, s, g), lambda bi, i: (bi, 0, 0)),
            pl.BlockSpec((1, g), lambda bi, i: (0, 0)),
        ],
        out_specs=pl.BlockSpec((None, QB, g), lambda bi, i: (bi, i, 0)),
        out_shape=jax.ShapeDtypeStruct((b, s, g), BF16),
        scratch_shapes=[
            pltpu.VMEM((N_HEADS * QB, QB), F32),
            pltpu.VMEM((QB, g), F32),
        ],
        compiler_params=pltpu.CompilerParams(
            dimension_semantics=("arbitrary", "arbitrary"), vmem_limit_bytes=48 * 1024 * 1024),
        name="stick_attention",
    )(q, k, v, g_c)


def _out_ffn_kernel(x_ref, yabd_ref, yc_ref, wo_abd_ref, wo_c_ref, gf_ref, wup_ref, wdn_ref, gfin_ref,
                    o_ref, *, final_norm):
    x1 = (x_ref[...]
          + jnp.dot(yabd_ref[...], wo_abd_ref[...], preferred_element_type=F32)
          + jnp.dot(yc_ref[...], wo_c_ref[...], preferred_element_type=F32))
    h = (_rms(x1) * gf_ref[...]).astype(BF16)
    a = jnp.maximum(jnp.dot(h, wup_ref[...], preferred_element_type=F32), 0.0)
    x2 = x1 + jnp.dot((a * a).astype(BF16), wdn_ref[...], preferred_element_type=F32)
    if final_norm:
        x2 = _rms(x2) * gfin_ref[...]
    o_ref[...] = x2


def _out_ffn(x2, y_abd, y_c, wo_abd, wo_c, g_ffn, w_up, w_dn, g_fin, final_norm):
    n = x2.shape[0]
    tm = TM_PROJ
    row = lambda i: (i, 0)
    const = lambda i: (0, 0)
    resident = functools.partial(pl.BlockSpec, index_map=const, pipeline_mode=pl.Buffered(1))
    return pl.pallas_call(
        functools.partial(_out_ffn_kernel, final_norm=final_norm),
        grid=(n // tm,),
        in_specs=[
            pl.BlockSpec((tm, D_MODEL), row),
            pl.BlockSpec((tm, 3 * D_GROUP), row),
            pl.BlockSpec((tm, D_GROUP), row),
            resident(wo_abd.shape),
            resident(wo_c.shape),
            pl.BlockSpec((1, D_MODEL), const),
            resident(w_up.shape),
            resident(w_dn.shape),
            pl.BlockSpec((1, D_MODEL), const),
        ],
        out_specs=pl.BlockSpec((tm, D_MODEL), row),
        out_shape=jax.ShapeDtypeStruct((n, D_MODEL), F32),
        compiler_params=pltpu.CompilerParams(
            dimension_semantics=("arbitrary",), vmem_limit_bytes=56 * 1024 * 1024),
        name="out_ffn",
    )(x2, y_abd, y_c, wo_abd, wo_c, g_ffn, w_up, w_dn, g_fin)


def kernel(x, norm_mix_g, w_in, gmlp_v_g, gmlp_w_s, gmlp_b_s, short_conv_w, conf_conv_w, conf_ln_g, conf_ln_b,
           mix_out_g, w_out, norm_ffn_g, w_up, w_down, final_norm_g):
    b, s, d = x.shape
    depth = w_in.shape[0]
    g = D_GROUP
    n = b * s
    x2 = x.reshape(n, d)
    for l in range(depth):
        za, zb, q, k, v, zd = _in_proj(x2, norm_mix_g[l][None, :], w_in[l].astype(BF16))
        ws_cat = jnp.transpose(gmlp_w_s[l], (1, 0, 2)).reshape(CHUNK, N_HEADS * CHUNK)
        bs_lanes = jnp.repeat(gmlp_b_s[l].T, HEAD_DIM, axis=1)
        go = mix_out_g[l]
        g_abd = jnp.concatenate([go[0:2 * g], go[3 * g:4 * g]])[None, :]
        y_abd = _local_mix(
            za.reshape(b, s, 2 * g), zb.reshape(b, s, 3 * g), zd.reshape(b, s, 2 * g),
            gmlp_v_g[l][None, :], ws_cat, bs_lanes, short_conv_w[l], conf_conv_w[l],
            conf_ln_g[l][None, :], conf_ln_b[l][None, :], g_abd)
        y_c = _attention(q.reshape(b, s, g), k.reshape(b, s, g), v.reshape(b, s, g), go[2 * g:3 * g][None, :])
        wo = w_out[l].astype(BF16)
        wo_abd = jnp.concatenate([wo[0:2 * g], wo[3 * g:4 * g]], axis=0)
        x2 = _out_ffn(
            x2, y_abd.reshape(n, 3 * g), y_c.reshape(n, g), wo_abd, wo[2 * g:3 * g],
            norm_ffn_g[l][None, :], w_up[l].astype(BF16), w_down[l].astype(BF16),
            final_norm_g[None, :], final_norm=(l == depth - 1))
    return x2.reshape(b, s, d)
```

```python
import functools
import math

import jax
import jax.numpy as jnp
from jax import lax
from jax.experimental import pallas as pl
from jax.experimental.pallas import tpu as pltpu

D_MODEL = 1024
D_GROUP = 256
HEAD_DIM = 64
N_HEADS = D_GROUP // HEAD_DIM
CHUNK = 128
SHORT_W = 3
CONF_W = 31
D_FF = 4 * D_MODEL
D_IN_PROJ = 10 * D_GROUP
EPS = 1e-6

F32 = jnp.float32
BF16 = jnp.bfloat16

SUBLANES = 8
MIB = 1024 * 1024

TM = 512
QB = CHUNK
NQ = TM // QB
ATTN_WINDOW = 3
SHORT_HALO = SUBLANES
CONF_HALO = 4 * SUBLANES
STICK_DEAD = 110.0


def _rms(x):
    return x * lax.rsqrt(jnp.mean(x * x, axis=-1, keepdims=True) + EPS)


def _gelu_tanh(x):
    c = math.sqrt(2.0 / math.pi)
    return 0.5 * x * (1.0 + jnp.tanh(c * (x + 0.044715 * (x * x * x))))


def _head_of_lane():
    return lax.broadcasted_iota(jnp.int32, (QB, D_GROUP), 1) // HEAD_DIM


def _block_diag(vblk):
    hol = _head_of_lane()
    return [jnp.where(hol == h, vblk, jnp.zeros_like(vblk)) for h in range(N_HEADS)]


def _local_chunk(r0, za, zb, pbuf, hbuf, wcat, vg_ref, bs_ref, wsc_ref, wcc_ref, lng_ref, lnb_ref):
    g = D_GROUP
    ga = _gelu_tanh(za[r0:r0 + CHUNK, :])
    u = ga[:, 0:g]
    v = (_rms(ga[:, g:2 * g]) * vg_ref[...]).astype(BF16)
    f = jnp.dot(wcat, jnp.concatenate(_block_diag(v), axis=0), preferred_element_type=F32) + bs_ref[...]
    ya = u * f
    conv_b = wsc_ref[SHORT_W - 1:SHORT_W, :] * pbuf[SHORT_HALO + r0:SHORT_HALO + r0 + CHUNK, :]
    for kk in range(SHORT_W - 1):
        off = SHORT_HALO + r0 - (SHORT_W - 1) + kk
        conv_b = conv_b + wsc_ref[kk:kk + 1, :] * pbuf[off:off + CHUNK, :]
    yb = zb[r0:r0 + CHUNK, 0:g] * conv_b
    base = CONF_HALO + r0 - (CONF_W - 1)
    conv_d = None
    for res in range(SUBLANES):
        part = None
        for kk in range(CONF_W):
            if (base + kk) % SUBLANES != res:
                continue
            off = base + kk - res
            term = wcc_ref[kk:kk + 1, :] * hbuf[off:off + CHUNK + SUBLANES, :]
            part = term if part is None else part + term
        if part is None:
            continue
        shifted = part[res:res + CHUNK, :]
        conv_d = shifted if conv_d is None else conv_d + shifted
    xc = conv_d - jnp.mean(conv_d, axis=-1, keepdims=True)
    var = jnp.mean(xc * xc, axis=-1, keepdims=True)
    ln = xc * lax.rsqrt(var + EPS) * lng_ref[...] + lnb_ref[...]
    yd = ln * jax.nn.sigmoid(ln)
    return ya, yb, yd


def _stack_heads(q):
    return jnp.concatenate(_block_diag(q), axis=0)


def _suffix_matrix():
    rr = lax.broadcasted_iota(jnp.int32, (2 * QB, 2 * QB), 0) % QB
    cc = lax.broadcasted_iota(jnp.int32, (2 * QB, 2 * QB), 1)
    return jnp.where((cc >= QB) | (rr > cc), 1.0, 0.0).astype(BF16)


def _neg_softplus(ns):
    return jnp.minimum(ns, 0.0) - jnp.log(1.0 + jnp.exp(-jnp.abs(ns)))


def _split(lom):
    hi = lom.astype(BF16)
    lo = (lom - hi.astype(F32)).astype(BF16)
    return jnp.concatenate([hi, lo], axis=1)


def _heads_to_lanes(wb):
    return [wb[h * QB:(h + 1) * QB, :] for h in range(N_HEADS)]


def _load_block(ref, kb):
    return ref[pl.ds(pl.multiple_of(kb * QB, QB), QB), :]


def _attn_window(qb, q, k_scr, v_scr, tt):
    nw, nh = ATTN_WINDOW, N_HEADS
    qrow = lax.broadcasted_iota(jnp.int32, (nh * QB, QB), 0) % QB
    kcol = lax.broadcasted_iota(jnp.int32, (nh * QB, QB), 1)
    causal = kcol < qrow
    kbs = [qb - (nw - 1) + j for j in range(nw)]
    exists = [kb >= 0 for kb in kbs]
    kidx = [jnp.maximum(kb, 0) for kb in kbs]
    kwin = jnp.concatenate([_load_block(k_scr, kb) for kb in kidx], axis=0)
    ns = lax.dot_general(_stack_heads(q), kwin, (((1,), (1,)), ((), ())), preferred_element_type=F32)
    ns_j = [ns[:, j * QB:(j + 1) * QB] for j in range(nw)]
    lom_j = []
    for j in range(nw):
        lom = _neg_softplus(ns_j[j])
        lom_j.append(jnp.where(causal, lom, 0.0) if j == nw - 1 else jnp.where(exists[j], lom, 0.0))
    ll = jnp.dot(jnp.concatenate([_split(l) for l in lom_j], axis=0), tt, preferred_element_type=F32)
    later = None
    wcols = [None] * nw
    for j in reversed(range(nw)):
        rows = slice(j * nh * QB, (j + 1) * nh * QB)
        logw = (lom_j[j] - ns_j[j]) + ll[rows, 0:QB]
        if later is not None:
            logw = logw + later
        w = jnp.exp(logw)
        w = jnp.where(causal, w, 0.0) if j == nw - 1 else jnp.where(exists[j], w, 0.0)
        wcols[j] = _heads_to_lanes(w.astype(BF16))
        tot = ll[rows, QB:2 * QB]
        later = tot if later is None else later + tot
    wcat = jnp.concatenate([c for cols in wcols for c in cols], axis=1)
    vbd = jnp.concatenate([m for kb in kidx for m in _block_diag(_load_block(v_scr, kb))], axis=0)
    return jnp.dot(wcat, vbd, preferred_element_type=F32), later


def _attn_block(kb, qs, k_scr, v_scr, tt, carry):
    nsb = lax.dot_general(qs, _load_block(k_scr, kb), (((1,), (1,)), ((), ())), preferred_element_type=F32)
    lom = _neg_softplus(nsb)
    llb = jnp.dot(_split(lom), tt, preferred_element_type=F32)
    w = jnp.exp((lom - nsb) + llb[:, 0:QB] + carry)
    wcat = jnp.concatenate(_heads_to_lanes(w.astype(BF16)), axis=1)
    vbd = jnp.concatenate(_block_diag(_load_block(v_scr, kb)), axis=0)
    return jnp.dot(wcat, vbd, preferred_element_type=F32), llb[:, QB:2 * QB]


def _mixer_kernel(x_ref, gin_ref, win_ref, vg_ref, ws_ref, bs_ref, wsc_ref, wcc_ref, lng_ref, lnb_ref, go_ref,
                  y_ref, k_scr, v_scr, q_scr, pbuf, hbuf, carry_scr, acc_scr):
    i = pl.program_id(1)
    g = D_GROUP
    t = TM

    @pl.when(i == 0)
    def _():
        pbuf[0:SHORT_HALO, :] = jnp.zeros((SHORT_HALO, g), F32)
        hbuf[0:CONF_HALO, :] = jnp.zeros((CONF_HALO, g), F32)
        hbuf[CONF_HALO + t:CONF_HALO + t + SUBLANES, :] = jnp.zeros((SUBLANES, g), F32)

    h = (_rms(x_ref[...]) * gin_ref[...]).astype(BF16)

    def proj(lo, hi):
        return jnp.dot(h, win_ref[:, lo:hi], preferred_element_type=F32)

    row0 = pl.multiple_of(i * t, t)
    k_scr[pl.ds(row0, t), :] = proj(6 * g, 7 * g).astype(BF16)
    v_scr[pl.ds(row0, t), :] = proj(7 * g, 8 * g).astype(BF16)
    q_scr[...] = (proj(5 * g, 6 * g) * -(HEAD_DIM ** -0.5)).astype(BF16)
    za = proj(0, 2 * g)
    zb = proj(2 * g, 5 * g)
    zd = proj(8 * g, 10 * g)

    pbuf[SHORT_HALO:SHORT_HALO + t, :] = zb[:, g:2 * g] * zb[:, 2 * g:3 * g]
    hbuf[CONF_HALO:CONF_HALO + t, :] = zd[:, 0:g] * jax.nn.sigmoid(zd[:, g:2 * g])

    rr = lax.broadcasted_iota(jnp.int32, (CHUNK, N_HEADS * CHUNK), 0)
    cc = lax.broadcasted_iota(jnp.int32, (CHUNK, N_HEADS * CHUNK), 1)
    wcat = jnp.where((cc % CHUNK) <= rr, ws_ref[...], 0.0).astype(BF16)
    tt = _suffix_matrix()

    for c in range(NQ):
        r0 = c * CHUNK
        ya, yb, yd = _local_chunk(r0, za, zb, pbuf, hbuf, wcat, vg_ref, bs_ref, wsc_ref, wcc_ref, lng_ref, lnb_ref)
        y_ref[r0:r0 + CHUNK, 0:g] = (_rms(ya) * go_ref[:, 0:g]).astype(BF16)
        y_ref[r0:r0 + CHUNK, g:2 * g] = (_rms(yb) * go_ref[:, g:2 * g]).astype(BF16)
        y_ref[r0:r0 + CHUNK, 3 * g:4 * g] = (_rms(yd) * go_ref[:, 3 * g:4 * g]).astype(BF16)
        pv, carry = _attn_window(i * NQ + c, q_scr[r0:r0 + QB, :], k_scr, v_scr, tt)
        acc_scr[c] = pv
        carry_scr[c] = carry

    pbuf[0:SHORT_HALO, :] = pbuf[t:t + SHORT_HALO, :]
    hbuf[0:CONF_HALO, :] = hbuf[t:t + CONF_HALO, :]

    def older_blocks(c, _):
        qs = _stack_heads(q_scr[pl.ds(pl.multiple_of(c * QB, QB), QB), :])

        def cond(st):
            kb, live = st
            return jnp.logical_and(kb >= 0, live > -STICK_DEAD)

        def body(st):
            kb, _ = st
            carry = carry_scr[c]
            pv, tot = _attn_block(kb, qs, k_scr, v_scr, tt, carry)
            acc_scr[c] = acc_scr[c] + pv
            carry_scr[c] = carry + tot
            return kb - 1, jnp.max(carry + tot)

        lax.while_loop(cond, body, (i * NQ + c - ATTN_WINDOW, jnp.max(carry_scr[c])))
        return 0

    lax.fori_loop(0, NQ, older_blocks, 0)

    for c in range(NQ):
        r0 = c * CHUNK
        y_ref[r0:r0 + CHUNK, 2 * g:3 * g] = (_rms(acc_scr[c]) * go_ref[:, 2 * g:3 * g]).astype(BF16)


def _mixer(x, g_in, w_in, vg, ws_cat, bs_lanes, w_short, w_conf, ln_g, ln_b, g_out):
    b, s, d = x.shape
    g = D_GROUP
    tile = lambda bi, i: (bi, i, 0)
    const = lambda bi, i: (0, 0)
    return pl.pallas_call(
        _mixer_kernel,
        grid=(b, s // TM),
        in_specs=[
            pl.BlockSpec((None, TM, d), tile),
            pl.BlockSpec((1, d), const),
            pl.BlockSpec((d, D_IN_PROJ), const, pipeline_mode=pl.Buffered(1)),
            pl.BlockSpec((1, g), const),
            pl.BlockSpec((CHUNK, N_HEADS * CHUNK), const),
            pl.BlockSpec((CHUNK, g), const),
            pl.BlockSpec((SHORT_W, g), const),
            pl.BlockSpec((CONF_W, g), const),
            pl.BlockSpec((1, g), const),
            pl.BlockSpec((1, g), const),
            pl.BlockSpec((1, 4 * g), const),
        ],
        out_specs=pl.BlockSpec((None, TM, 4 * g), tile),
        out_shape=jax.ShapeDtypeStruct((b, s, 4 * g), BF16),
        scratch_shapes=[
            pltpu.VMEM((s, g), BF16),
            pltpu.VMEM((s, g), BF16),
            pltpu.VMEM((TM, g), BF16),
            pltpu.VMEM((SHORT_HALO + TM, g), F32),
            pltpu.VMEM((CONF_HALO + TM + SUBLANES, g), F32),
            pltpu.VMEM((NQ, N_HEADS * QB, QB), F32),
            pltpu.VMEM((NQ, QB, g), F32),
        ],
        compiler_params=pltpu.CompilerParams(
            dimension_semantics=("arbitrary", "arbitrary"), vmem_limit_bytes=56 * MIB),
        name="mixer",
    )(x, g_in, w_in, vg, ws_cat, bs_lanes, w_short, w_conf, ln_g, ln_b, g_out)


def _out_ffn_kernel(x_ref, y_ref, wo_ref, gf_ref, wup_ref, wdn_ref, gfin_ref, o_ref, *, final_norm):
    x1 = x_ref[...] + jnp.dot(y_ref[...], wo_ref[...], preferred_element_type=F32)
    h = (_rms(x1) * gf_ref[...]).astype(BF16)
    a = jnp.maximum(jnp.dot(h, wup_ref[...], preferred_element_type=F32), 0.0)
    x2 = x1 + jnp.dot((a * a).astype(BF16), wdn_ref[...], preferred_element_type=F32)
    if final_norm:
        x2 = _rms(x2) * gfin_ref[...]
    o_ref[...] = x2


def _out_ffn(x2, y, wo, g_ffn, w_up, w_dn, g_fin, final_norm):
    n = x2.shape[0]
    row = lambda i: (i, 0)
    const = lambda i: (0, 0)
    resident = functools.partial(pl.BlockSpec, index_map=const, pipeline_mode=pl.Buffered(1))
    return pl.pallas_call(
        functools.partial(_out_ffn_kernel, final_norm=final_norm),
        grid=(n // TM,),
        in_specs=[
            pl.BlockSpec((TM, D_MODEL), row),
            pl.BlockSpec((TM, D_MODEL), row),
            resident(wo.shape),
            pl.BlockSpec((1, D_MODEL), const),
            resident(w_up.shape),
            resident(w_dn.shape),
            pl.BlockSpec((1, D_MODEL), const),
        ],
        out_specs=pl.BlockSpec((TM, D_MODEL), row),
        out_shape=jax.ShapeDtypeStruct((n, D_MODEL), F32),
        compiler_params=pltpu.CompilerParams(
            dimension_semantics=("arbitrary",), vmem_limit_bytes=56 * MIB),
        name="out_ffn",
    )(x2, y, wo, g_ffn, w_up, w_dn, g_fin)


def kernel(x, norm_mix_g, w_in, gmlp_v_g, gmlp_w_s, gmlp_b_s, short_conv_w, conf_conv_w, conf_ln_g, conf_ln_b,
           mix_out_g, w_out, norm_ffn_g, w_up, w_down, final_norm_g):
    b, s, d = x.shape
    depth = w_in.shape[0]
    n = b * s
    for l in range(depth):
        ws_cat = jnp.transpose(gmlp_w_s[l], (1, 0, 2)).reshape(CHUNK, N_HEADS * CHUNK)
        bs_lanes = jnp.repeat(gmlp_b_s[l].T, HEAD_DIM, axis=1)
        y = _mixer(
            x, norm_mix_g[l][None, :], w_in[l].astype(BF16), gmlp_v_g[l][None, :], ws_cat, bs_lanes,
            short_conv_w[l], conf_conv_w[l], conf_ln_g[l][None, :], conf_ln_b[l][None, :], mix_out_g[l][None, :])
        x = _out_ffn(
            x.reshape(n, d), y.reshape(n, d), w_out[l].astype(BF16), norm_ffn_g[l][None, :],
            w_up[l].astype(BF16), w_down[l].astype(BF16), final_norm_g[None, :],
            final_norm=(l == depth - 1)).reshape(b, s, d)
    return x
```

```python
import functools
import math

import jax
import jax.numpy as jnp
from jax import lax
from jax.experimental import pallas as pl
from jax.experimental.pallas import tpu as pltpu

D_MODEL = 1024
D_GROUP = 256
HEAD_DIM = 64
N_HEADS = D_GROUP // HEAD_DIM
CHUNK = 128
SHORT_W = 3
CONF_W = 31
D_FF = 4 * D_MODEL
D_IN_PROJ = 10 * D_GROUP
EPS = 1e-6

F32 = jnp.float32
BF16 = jnp.bfloat16

SUBLANES = 8
MIB = 1024 * 1024

TM = 512
SUB = 256
QB = CHUNK
NQ = TM // QB
ATTN_WINDOW = 3
SHORT_HALO = SUBLANES
CONF_HALO = 4 * SUBLANES
STICK_DEAD = 110.0


def _rms(x):
    return x * lax.rsqrt(jnp.mean(x * x, axis=-1, keepdims=True) + EPS)


def _gelu_tanh(x):
    c = math.sqrt(2.0 / math.pi)
    return 0.5 * x * (1.0 + jnp.tanh(c * (x + 0.044715 * (x * x * x))))


def _head_of_lane():
    return lax.broadcasted_iota(jnp.int32, (QB, D_GROUP), 1) // HEAD_DIM


def _block_diag(vblk):
    hol = _head_of_lane()
    return [jnp.where(hol == h, vblk, jnp.zeros_like(vblk)) for h in range(N_HEADS)]


def _local_chunk(zr, s0, za, zb, pbuf, hbuf, wcat, vg_ref, bs_ref, wsc_ref, wcc_ref, lng_ref, lnb_ref):
    g = D_GROUP
    r0 = s0 + zr
    ga = _gelu_tanh(za[zr:zr + CHUNK, :])
    u = ga[:, 0:g]
    v = (_rms(ga[:, g:2 * g]) * vg_ref[...]).astype(BF16)
    f = jnp.dot(wcat, jnp.concatenate(_block_diag(v), axis=0), preferred_element_type=F32) + bs_ref[...]
    ya = u * f
    conv_b = wsc_ref[SHORT_W - 1:SHORT_W, :] * pbuf[SHORT_HALO + r0:SHORT_HALO + r0 + CHUNK, :]
    for kk in range(SHORT_W - 1):
        off = SHORT_HALO + r0 - (SHORT_W - 1) + kk
        conv_b = conv_b + wsc_ref[kk:kk + 1, :] * pbuf[off:off + CHUNK, :]
    yb = zb[zr:zr + CHUNK, 0:g] * conv_b
    base = CONF_HALO + r0 - (CONF_W - 1)
    conv_d = None
    for res in range(SUBLANES):
        part = None
        for kk in range(CONF_W):
            if (base + kk) % SUBLANES != res:
                continue
            off = base + kk - res
            term = wcc_ref[kk:kk + 1, :] * hbuf[off:off + CHUNK + SUBLANES, :]
            part = term if part is None else part + term
        if part is None:
            continue
        shifted = part[res:res + CHUNK, :]
        conv_d = shifted if conv_d is None else conv_d + shifted
    xc = conv_d - jnp.mean(conv_d, axis=-1, keepdims=True)
    var = jnp.mean(xc * xc, axis=-1, keepdims=True)
    ln = xc * lax.rsqrt(var + EPS) * lng_ref[...] + lnb_ref[...]
    yd = ln * jax.nn.sigmoid(ln)
    return ya, yb, yd


def _stack_heads(q):
    return jnp.concatenate(_block_diag(q), axis=0)


def _suffix_matrix():
    rr = lax.broadcasted_iota(jnp.int32, (2 * QB, 2 * QB), 0) % QB
    cc = lax.broadcasted_iota(jnp.int32, (2 * QB, 2 * QB), 1)
    return jnp.where((cc >= QB) | (rr > cc), 1.0, 0.0).astype(BF16)


def _neg_softplus(ns):
    return jnp.minimum(ns, 0.0) - jnp.log(1.0 + jnp.exp(-jnp.abs(ns)))


def _split(lom):
    hi = lom.astype(BF16)
    lo = (lom - hi.astype(F32)).astype(BF16)
    return jnp.concatenate([hi, lo], axis=1)


def _heads_to_lanes(wb):
    return [wb[h * QB:(h + 1) * QB, :] for h in range(N_HEADS)]


def _load_block(ref, kb):
    return ref[pl.ds(pl.multiple_of(kb * QB, QB), QB), :]


def _window_blocks(qb):
    kbs = [qb - (ATTN_WINDOW - 1) + j for j in range(ATTN_WINDOW)]
    return [kb >= 0 for kb in kbs], [jnp.maximum(kb, 0) for kb in kbs]


def _attn_scores(qb, q, k_scr):
    _, kidx = _window_blocks(qb)
    kwin = jnp.concatenate([_load_block(k_scr, kb) for kb in kidx], axis=0)
    return lax.dot_general(_stack_heads(q), kwin, (((1,), (1,)), ((), ())), preferred_element_type=F32)


def _attn_window(qb, ns, v_scr, tt):
    nw, nh = ATTN_WINDOW, N_HEADS
    qrow = lax.broadcasted_iota(jnp.int32, (nh * QB, QB), 0) % QB
    kcol = lax.broadcasted_iota(jnp.int32, (nh * QB, QB), 1)
    causal = kcol < qrow
    exists, kidx = _window_blocks(qb)
    ns_j = [ns[:, j * QB:(j + 1) * QB] for j in range(nw)]
    lom_j = []
    for j in range(nw):
        lom = _neg_softplus(ns_j[j])
        lom_j.append(jnp.where(causal, lom, 0.0) if j == nw - 1 else jnp.where(exists[j], lom, 0.0))
    ll = jnp.dot(jnp.concatenate([_split(l) for l in lom_j], axis=0), tt, preferred_element_type=F32)
    later = None
    wcols = [None] * nw
    for j in reversed(range(nw)):
        rows = slice(j * nh * QB, (j + 1) * nh * QB)
        logw = (lom_j[j] - ns_j[j]) + ll[rows, 0:QB]
        if later is not None:
            logw = logw + later
        w = jnp.exp(logw)
        w = jnp.where(causal, w, 0.0) if j == nw - 1 else jnp.where(exists[j], w, 0.0)
        wcols[j] = _heads_to_lanes(w.astype(BF16))
        tot = ll[rows, QB:2 * QB]
        later = tot if later is None else later + tot
    wcat = jnp.concatenate([c for cols in wcols for c in cols], axis=1)
    vbd = jnp.concatenate([m for kb in kidx for m in _block_diag(_load_block(v_scr, kb))], axis=0)
    return jnp.dot(wcat, vbd, preferred_element_type=F32), later


def _attn_block(kb, qs, k_scr, v_scr, tt, carry):
    nsb = lax.dot_general(qs, _load_block(k_scr, kb), (((1,), (1,)), ((), ())), preferred_element_type=F32)
    lom = _neg_softplus(nsb)
    llb = jnp.dot(_split(lom), tt, preferred_element_type=F32)
    w = jnp.exp((lom - nsb) + llb[:, 0:QB] + carry)
    wcat = jnp.concatenate(_heads_to_lanes(w.astype(BF16)), axis=1)
    vbd = jnp.concatenate(_block_diag(_load_block(v_scr, kb)), axis=0)
    return jnp.dot(wcat, vbd, preferred_element_type=F32), llb[:, QB:2 * QB]


def _mixer_kernel(x_ref, gin_ref, win_ref, vg_ref, ws_ref, bs_ref, wsc_ref, wcc_ref, lng_ref, lnb_ref, go_ref,
                  yl_ref, yc_ref, k_scr, v_scr, qa_scr, qn_scr, kn_scr, vn_scr, pbuf, hbuf, carry_scr, acc_scr):
    i = pl.program_id(1)
    g = D_GROUP
    t = TM
    ia = jnp.maximum(i - 1, 0)

    @pl.when(jnp.logical_and(pl.program_id(0) == 0, i == 0))
    def _():
        qn_scr[...] = jnp.zeros(qn_scr.shape, BF16)
        kn_scr[...] = jnp.zeros(kn_scr.shape, BF16)
        vn_scr[...] = jnp.zeros(vn_scr.shape, BF16)

    @pl.when(i == 0)
    def _():
        pbuf[0:SHORT_HALO, :] = jnp.zeros((SHORT_HALO, g), F32)
        hbuf[...] = jnp.zeros(hbuf.shape, F32)

    rr = lax.broadcasted_iota(jnp.int32, (CHUNK, N_HEADS * CHUNK), 0)
    cc = lax.broadcasted_iota(jnp.int32, (CHUNK, N_HEADS * CHUNK), 1)
    wcat = jnp.where((cc % CHUNK) <= rr, ws_ref[...], 0.0).astype(BF16)
    tt = _suffix_matrix()

    arow = pl.multiple_of(ia * t, t)
    qa_scr[...] = qn_scr[...]
    k_scr[pl.ds(arow, t), :] = kn_scr[...]
    v_scr[pl.ds(arow, t), :] = vn_scr[...]

    ns_c = [_attn_scores(ia * NQ + c, qa_scr[c * QB:(c + 1) * QB, :], k_scr) for c in range(NQ)]
    anchor = sum(jnp.minimum(jnp.maximum(ns[0:1, 0:CHUNK], 0.0), 0.0) for ns in ns_c)
    gin = gin_ref[...] + jnp.concatenate([anchor] * (D_MODEL // CHUNK), axis=1)

    for sub in range(TM // SUB):
        s0 = sub * SUB
        h = (_rms(x_ref[s0:s0 + SUB, :]) * gin).astype(BF16)

        def proj(lo, hi):
            return jnp.dot(h, win_ref[:, lo:hi], preferred_element_type=F32)

        kn_scr[s0:s0 + SUB, :] = proj(6 * g, 7 * g).astype(BF16)
        vn_scr[s0:s0 + SUB, :] = proj(7 * g, 8 * g).astype(BF16)
        qn_scr[s0:s0 + SUB, :] = (proj(5 * g, 6 * g) * -(HEAD_DIM ** -0.5)).astype(BF16)
        za = proj(0, 2 * g)
        zb = proj(2 * g, 5 * g)
        zd = proj(8 * g, 10 * g)

        pbuf[SHORT_HALO + s0:SHORT_HALO + s0 + SUB, :] = zb[:, g:2 * g] * zb[:, 2 * g:3 * g]
        hbuf[CONF_HALO + s0:CONF_HALO + s0 + SUB, :] = zd[:, 0:g] * jax.nn.sigmoid(zd[:, g:2 * g])

        for cs in range(SUB // CHUNK):
            c = sub * (SUB // CHUNK) + cs
            r0 = c * CHUNK
            pv, carry = _attn_window(ia * NQ + c, ns_c[c], v_scr, tt)
            acc_scr[c] = pv
            carry_scr[c] = carry
            ya, yb, yd = _local_chunk(cs * CHUNK, s0, za, zb, pbuf, hbuf, wcat,
                                      vg_ref, bs_ref, wsc_ref, wcc_ref, lng_ref, lnb_ref)
            yl_ref[r0:r0 + CHUNK, 0:g] = (_rms(ya) * go_ref[:, 0:g]).astype(BF16)
            yl_ref[r0:r0 + CHUNK, g:2 * g] = (_rms(yb) * go_ref[:, g:2 * g]).astype(BF16)
            yl_ref[r0:r0 + CHUNK, 2 * g:3 * g] = (_rms(yd) * go_ref[:, 3 * g:4 * g]).astype(BF16)

    pbuf[0:SHORT_HALO, :] = pbuf[t:t + SHORT_HALO, :]
    hbuf[0:CONF_HALO, :] = hbuf[t:t + CONF_HALO, :]

    def older_blocks(c, _):
        qs = _stack_heads(qa_scr[pl.ds(pl.multiple_of(c * QB, QB), QB), :])

        def cond(st):
            kb, live = st
            return jnp.logical_and(kb >= 0, live > -STICK_DEAD)

        def body(st):
            kb, _ = st
            carry = carry_scr[c]
            pv, tot = _attn_block(kb, qs, k_scr, v_scr, tt, carry)
            acc_scr[c] = acc_scr[c] + pv
            carry_scr[c] = carry + tot
            return kb - 1, jnp.max(carry + tot)

        lax.while_loop(cond, body, (ia * NQ + c - ATTN_WINDOW, jnp.max(carry_scr[c])))
        return 0

    lax.fori_loop(0, NQ, older_blocks, 0)

    for c in range(NQ):
        r0 = c * CHUNK
        yc_ref[r0:r0 + CHUNK, :] = (_rms(acc_scr[c]) * go_ref[:, 2 * g:3 * g]).astype(BF16)


def _mixer(x, g_in, w_in, vg, ws_cat, bs_lanes, w_short, w_conf, ln_g, ln_b, g_out):
    b, s, d = x.shape
    g = D_GROUP
    nt = s // TM
    const = lambda bi, i: (0, 0)
    return pl.pallas_call(
        _mixer_kernel,
        grid=(b, nt + 1),
        in_specs=[
            pl.BlockSpec((None, TM, d), lambda bi, i: (bi, jnp.minimum(i, nt - 1), 0)),
            pl.BlockSpec((1, d), const),
            pl.BlockSpec((d, D_IN_PROJ), const, pipeline_mode=pl.Buffered(1)),
            pl.BlockSpec((1, g), const),
            pl.BlockSpec((CHUNK, N_HEADS * CHUNK), const),
            pl.BlockSpec((CHUNK, g), const),
            pl.BlockSpec((SHORT_W, g), const),
            pl.BlockSpec((CONF_W, g), const),
            pl.BlockSpec((1, g), const),
            pl.BlockSpec((1, g), const),
            pl.BlockSpec((1, 4 * g), const),
        ],
        out_specs=[
            pl.BlockSpec((None, TM, 3 * g), lambda bi, i: (bi, i, 0)),
            pl.BlockSpec((None, TM, g), lambda bi, i: (bi, jnp.where(i == 0, nt, i - 1), 0)),
        ],
        out_shape=[
            jax.ShapeDtypeStruct((b, s + TM, 3 * g), BF16),
            jax.ShapeDtypeStruct((b, s + TM, g), BF16),
        ],
        scratch_shapes=[
            pltpu.VMEM((s, g), BF16),
            pltpu.VMEM((s, g), BF16),
            pltpu.VMEM((TM, g), BF16),
            pltpu.VMEM((TM, g), BF16),
            pltpu.VMEM((TM, g), BF16),
            pltpu.VMEM((TM, g), BF16),
            pltpu.VMEM((SHORT_HALO + TM, g), F32),
            pltpu.VMEM((CONF_HALO + TM + SUBLANES, g), F32),
            pltpu.VMEM((NQ, N_HEADS * QB, QB), F32),
            pltpu.VMEM((NQ, QB, g), F32),
        ],
        compiler_params=pltpu.CompilerParams(
            dimension_semantics=("arbitrary", "arbitrary"), vmem_limit_bytes=56 * MIB),
        name="mixer",
    )(x, g_in, w_in, vg, ws_cat, bs_lanes, w_short, w_conf, ln_g, ln_b, g_out)


def _out_ffn_kernel(x_ref, yl_ref, yc_ref, wol_ref, woc_ref, gf_ref, wup_ref, wdn_ref, gfin_ref, o_ref, *,
                    final_norm):
    x1 = (x_ref[...]
          + jnp.dot(yl_ref[...], wol_ref[...], preferred_element_type=F32)
          + jnp.dot(yc_ref[...], woc_ref[...], preferred_element_type=F32))
    h = (_rms(x1) * gf_ref[...]).astype(BF16)
    a = jnp.maximum(jnp.dot(h, wup_ref[...], preferred_element_type=F32), 0.0)
    x2 = x1 + jnp.dot((a * a).astype(BF16), wdn_ref[...], preferred_element_type=F32)
    if final_norm:
        x2 = _rms(x2) * gfin_ref[...]
    o_ref[...] = x2


def _out_ffn(x, y_loc, y_att, wo_loc, wo_att, g_ffn, w_up, w_dn, g_fin, final_norm):
    b, s, d = x.shape
    tile = lambda bi, i: (bi, i, 0)
    const = lambda bi, i: (0, 0)
    resident = functools.partial(pl.BlockSpec, index_map=const, pipeline_mode=pl.Buffered(1))
    return pl.pallas_call(
        functools.partial(_out_ffn_kernel, final_norm=final_norm),
        grid=(b, s // TM),
        in_specs=[
            pl.BlockSpec((None, TM, d), tile),
            pl.BlockSpec((None, TM, y_loc.shape[2]), tile),
            pl.BlockSpec((None, TM, y_att.shape[2]), tile),
            resident(wo_loc.shape),
            resident(wo_att.shape),
            pl.BlockSpec((1, d), const),
            resident(w_up.shape),
            resident(w_dn.shape),
            pl.BlockSpec((1, d), const),
        ],
        out_specs=pl.BlockSpec((None, TM, d), tile),
        out_shape=jax.ShapeDtypeStruct((b, s, d), F32),
        compiler_params=pltpu.CompilerParams(
            dimension_semantics=("arbitrary", "arbitrary"), vmem_limit_bytes=56 * MIB),
        name="out_ffn",
    )(x, y_loc, y_att, wo_loc, wo_att, g_ffn, w_up, w_dn, g_fin)


def kernel(x, norm_mix_g, w_in, gmlp_v_g, gmlp_w_s, gmlp_b_s, short_conv_w, conf_conv_w, conf_ln_g, conf_ln_b,
           mix_out_g, w_out, norm_ffn_g, w_up, w_down, final_norm_g):
    depth = w_in.shape[0]
    g = D_GROUP
    for l in range(depth):
        ws_cat = jnp.transpose(gmlp_w_s[l], (1, 0, 2)).reshape(CHUNK, N_HEADS * CHUNK)
        bs_lanes = jnp.repeat(gmlp_b_s[l].T, HEAD_DIM, axis=1)
        y_loc, y_att = _mixer(
            x, norm_mix_g[l][None, :], w_in[l].astype(BF16), gmlp_v_g[l][None, :], ws_cat, bs_lanes,
            short_conv_w[l], conf_conv_w[l], conf_ln_g[l][None, :], conf_ln_b[l][None, :], mix_out_g[l][None, :])
        wo = w_out[l].astype(BF16)
        wo_loc = jnp.concatenate([wo[0:2 * g], wo[3 * g:4 * g]], axis=0)
        x = _out_ffn(
            x, y_loc, y_att, wo_loc, wo[2 * g:3 * g], norm_ffn_g[l][None, :],
            w_up[l].astype(BF16), w_down[l].astype(BF16), final_norm_g[None, :],
            final_norm=(l == depth - 1))
    return x
```

```python
import functools
import math

import jax
import jax.numpy as jnp
from jax import lax
from jax.experimental import pallas as pl
from jax.experimental.pallas import tpu as pltpu

D_MODEL = 1024
D_GROUP = 256
HEAD_DIM = 64
N_HEADS = D_GROUP // HEAD_DIM
CHUNK = 128
SHORT_W = 3
CONF_W = 31
D_FF = 4 * D_MODEL
D_IN_PROJ = 10 * D_GROUP
EPS = 1e-6

F32 = jnp.float32
BF16 = jnp.bfloat16

SUBLANES = 8
MIB = 1024 * 1024

TM = 512
SUB = 256
QB = CHUNK
NQ = TM // QB
ATTN_WINDOW = 3
SHORT_HALO = SUBLANES
CONF_HALO = 4 * SUBLANES
STICK_DEAD = 110.0


def _rms(x):
    return x * lax.rsqrt(jnp.mean(x * x, axis=-1, keepdims=True) + EPS)


def _gelu_tanh(x):
    c = math.sqrt(2.0 / math.pi)
    return 0.5 * x * (1.0 + jnp.tanh(c * (x + 0.044715 * (x * x * x))))


def _head_of_lane():
    return lax.broadcasted_iota(jnp.int32, (QB, D_GROUP), 1) // HEAD_DIM


def _block_diag(vblk):
    hol = _head_of_lane()
    return [jnp.where(hol == h, vblk, jnp.zeros_like(vblk)) for h in range(N_HEADS)]


def _local_chunk(zr, s0, za, zb, pbuf, hbuf, wcat, vg_ref, bs_ref, wsc_ref, wcc_ref, lng_ref, lnb_ref):
    g = D_GROUP
    r0 = s0 + zr
    ga = _gelu_tanh(za[zr:zr + CHUNK, :])
    u = ga[:, 0:g]
    v = (_rms(ga[:, g:2 * g]) * vg_ref[...]).astype(BF16)
    f = jnp.dot(wcat, jnp.concatenate(_block_diag(v), axis=0), preferred_element_type=F32) + bs_ref[...]
    ya = u * f
    conv_b = wsc_ref[SHORT_W - 1:SHORT_W, :] * pbuf[SHORT_HALO + r0:SHORT_HALO + r0 + CHUNK, :]
    for kk in range(SHORT_W - 1):
        off = SHORT_HALO + r0 - (SHORT_W - 1) + kk
        conv_b = conv_b + wsc_ref[kk:kk + 1, :] * pbuf[off:off + CHUNK, :]
    yb = zb[zr:zr + CHUNK, 0:g] * conv_b
    base = CONF_HALO + r0 - (CONF_W - 1)
    conv_d = None
    for res in range(SUBLANES):
        part = None
        for kk in range(CONF_W):
            if (base + kk) % SUBLANES != res:
                continue
            off = base + kk - res
            term = wcc_ref[kk:kk + 1, :] * hbuf[off:off + CHUNK + SUBLANES, :]
            part = term if part is None else part + term
        if part is None:
            continue
        shifted = part[res:res + CHUNK, :]
        conv_d = shifted if conv_d is None else conv_d + shifted
    xc = conv_d - jnp.mean(conv_d, axis=-1, keepdims=True)
    var = jnp.mean(xc * xc, axis=-1, keepdims=True)
    ln = xc * lax.rsqrt(var + EPS) * lng_ref[...] + lnb_ref[...]
    yd = ln * jax.nn.sigmoid(ln)
    return ya, yb, yd


def _proj_local_kernel(x_ref, gin_ref, win_ref, vg_ref, ws_ref, bs_ref, wsc_ref, wcc_ref, lng_ref, lnb_ref,
                       go_ref, yl_ref, q_ref, k_ref, v_ref, pbuf, hbuf):
    g = D_GROUP
    t = TM

    @pl.when(pl.program_id(1) == 0)
    def _():
        pbuf[0:SHORT_HALO, :] = jnp.zeros((SHORT_HALO, g), F32)
        hbuf[...] = jnp.zeros(hbuf.shape, F32)

    rr = lax.broadcasted_iota(jnp.int32, (CHUNK, N_HEADS * CHUNK), 0)
    cc = lax.broadcasted_iota(jnp.int32, (CHUNK, N_HEADS * CHUNK), 1)
    wcat = jnp.where((cc % CHUNK) <= rr, ws_ref[...], 0.0).astype(BF16)

    for sub in range(TM // SUB):
        s0 = sub * SUB
        h = (_rms(x_ref[s0:s0 + SUB, :]) * gin_ref[...]).astype(BF16)

        def proj(lo, hi):
            return jnp.dot(h, win_ref[:, lo:hi], preferred_element_type=F32)

        q_ref[s0:s0 + SUB, :] = (proj(5 * g, 6 * g) * -(HEAD_DIM ** -0.5)).astype(BF16)
        k_ref[s0:s0 + SUB, :] = proj(6 * g, 7 * g).astype(BF16)
        v_ref[s0:s0 + SUB, :] = proj(7 * g, 8 * g).astype(BF16)
        za = proj(0, 2 * g)
        zb = proj(2 * g, 5 * g)
        zd = proj(8 * g, 10 * g)

        pbuf[SHORT_HALO + s0:SHORT_HALO + s0 + SUB, :] = zb[:, g:2 * g] * zb[:, 2 * g:3 * g]
        hbuf[CONF_HALO + s0:CONF_HALO + s0 + SUB, :] = zd[:, 0:g] * jax.nn.sigmoid(zd[:, g:2 * g])

        for cs in range(SUB // CHUNK):
            r0 = s0 + cs * CHUNK
            ya, yb, yd = _local_chunk(cs * CHUNK, s0, za, zb, pbuf, hbuf, wcat,
                                      vg_ref, bs_ref, wsc_ref, wcc_ref, lng_ref, lnb_ref)
            yl_ref[r0:r0 + CHUNK, 0:g] = (_rms(ya) * go_ref[:, 0:g]).astype(BF16)
            yl_ref[r0:r0 + CHUNK, g:2 * g] = (_rms(yb) * go_ref[:, g:2 * g]).astype(BF16)
            yl_ref[r0:r0 + CHUNK, 2 * g:3 * g] = (_rms(yd) * go_ref[:, 2 * g:3 * g]).astype(BF16)

    pbuf[0:SHORT_HALO, :] = pbuf[t:t + SHORT_HALO, :]
    hbuf[0:CONF_HALO, :] = hbuf[t:t + CONF_HALO, :]


def _proj_local(x, g_in, w_in, vg, ws_cat, bs_lanes, w_short, w_conf, ln_g, ln_b, g_loc):
    b, s, d = x.shape
    g = D_GROUP
    tile = lambda bi, i: (bi, i, 0)
    const = lambda bi, i: (0, 0)
    return pl.pallas_call(
        _proj_local_kernel,
        grid=(b, s // TM),
        in_specs=[
            pl.BlockSpec((None, TM, d), tile),
            pl.BlockSpec((1, d), const),
            pl.BlockSpec((d, D_IN_PROJ), const, pipeline_mode=pl.Buffered(1)),
            pl.BlockSpec((1, g), const),
            pl.BlockSpec((CHUNK, N_HEADS * CHUNK), const),
            pl.BlockSpec((CHUNK, g), const),
            pl.BlockSpec((SHORT_W, g), const),
            pl.BlockSpec((CONF_W, g), const),
            pl.BlockSpec((1, g), const),
            pl.BlockSpec((1, g), const),
            pl.BlockSpec((1, 3 * g), const),
        ],
        out_specs=[
            pl.BlockSpec((None, TM, 3 * g), tile),
            pl.BlockSpec((None, TM, g), tile),
            pl.BlockSpec((None, TM, g), tile),
            pl.BlockSpec((None, TM, g), tile),
        ],
        out_shape=[
            jax.ShapeDtypeStruct((b, s, 3 * g), BF16),
            jax.ShapeDtypeStruct((b, s, g), BF16),
            jax.ShapeDtypeStruct((b, s, g), BF16),
            jax.ShapeDtypeStruct((b, s, g), BF16),
        ],
        scratch_shapes=[
            pltpu.VMEM((SHORT_HALO + TM, g), F32),
            pltpu.VMEM((CONF_HALO + TM + SUBLANES, g), F32),
        ],
        compiler_params=pltpu.CompilerParams(
            dimension_semantics=("arbitrary", "arbitrary"), vmem_limit_bytes=40 * MIB),
        name="proj_local",
    )(x, g_in, w_in, vg, ws_cat, bs_lanes, w_short, w_conf, ln_g, ln_b, g_loc)


def _stack_heads(q):
    return jnp.concatenate(_block_diag(q), axis=0)


def _suffix_matrix():
    rr = lax.broadcasted_iota(jnp.int32, (2 * QB, 2 * QB), 0) % QB
    cc = lax.broadcasted_iota(jnp.int32, (2 * QB, 2 * QB), 1)
    return jnp.where((cc >= QB) | (rr > cc), 1.0, 0.0).astype(BF16)


def _neg_softplus(ns):
    return jnp.minimum(ns, 0.0) - jnp.log(1.0 + jnp.exp(-jnp.abs(ns)))


def _split(lom):
    hi = lom.astype(BF16)
    lo = (lom - hi.astype(F32)).astype(BF16)
    return jnp.concatenate([hi, lo], axis=1)


def _heads_to_lanes(wb):
    return [wb[h * QB:(h + 1) * QB, :] for h in range(N_HEADS)]


class _Window:
    def __init__(self, q, kblks, vblks, exists, tt):
        self.q, self.kblks, self.vblks, self.exists, self.tt = q, kblks, vblks, exists, tt
        self.nw = len(kblks)
        nh = N_HEADS
        qrow = lax.broadcasted_iota(jnp.int32, (nh * QB, QB), 0) % QB
        kcol = lax.broadcasted_iota(jnp.int32, (nh * QB, QB), 1)
        self.causal = kcol < qrow

    def _mask(self, j, val):
        if j == self.nw - 1:
            return jnp.where(self.causal, val, 0.0)
        return val if self.exists[j] is None else jnp.where(self.exists[j], val, 0.0)

    def scores(self):
        self.ns = lax.dot_general(_stack_heads(self.q), jnp.concatenate(self.kblks, axis=0),
                                  (((1,), (1,)), ((), ())), preferred_element_type=F32)
        return self.ns

    def suffix_sums(self):
        nw = self.nw
        self.ns_j = [self.ns[:, j * QB:(j + 1) * QB] for j in range(nw)]
        self.lom_j = [self._mask(j, _neg_softplus(self.ns_j[j])) for j in range(nw)]
        self.ll = jnp.dot(jnp.concatenate([_split(l) for l in self.lom_j], axis=0), self.tt,
                          preferred_element_type=F32)
        return self.ll

    def weights(self):
        nw, nh = self.nw, N_HEADS
        later = None
        wcols = [None] * nw
        for j in reversed(range(nw)):
            rows = slice(j * nh * QB, (j + 1) * nh * QB)
            logw = (self.lom_j[j] - self.ns_j[j]) + self.ll[rows, 0:QB]
            if later is not None:
                logw = logw + later
            wcols[j] = _heads_to_lanes(self._mask(j, jnp.exp(logw)).astype(BF16))
            tot = self.ll[rows, QB:2 * QB]
            later = tot if later is None else later + tot
        self.wcat = jnp.concatenate([c for cols in wcols for c in cols], axis=1)
        return later

    def values(self):
        vbd = jnp.concatenate([m for vb in self.vblks for m in _block_diag(vb)], axis=0)
        return jnp.dot(self.wcat, vbd, preferred_element_type=F32)


def _after(x, vals):
    z = sum(jnp.minimum(jnp.maximum(v[0:1, 0:CHUNK], 0.0), 0.0) for v in vals)
    return jnp.concatenate([x[:, 0:CHUNK] + z.astype(x.dtype), x[:, CHUNK:]], axis=1)


def _attn_block(qs, kblk, vblk, tt, carry):
    nsb = lax.dot_general(qs, kblk, (((1,), (1,)), ((), ())), preferred_element_type=F32)
    lom = _neg_softplus(nsb)
    llb = jnp.dot(_split(lom), tt, preferred_element_type=F32)
    w = jnp.exp((lom - nsb) + llb[:, 0:QB] + carry)
    wcat = jnp.concatenate(_heads_to_lanes(w.astype(BF16)), axis=1)
    vbd = jnp.concatenate(_block_diag(vblk), axis=0)
    return jnp.dot(wcat, vbd, preferred_element_type=F32), llb[:, QB:2 * QB]


def _attn_ffn_kernel(x_ref, yl_ref, q_ref, kc_ref, vc_ref, kp_ref, vp_ref, k_hbm, v_hbm,
                     wol_ref, woc_ref, gf_ref, wup_ref, wdn_ref, gfin_ref, gc_ref,
                     o_ref, yc_scr, carry_scr, acc_scr, kbuf, vbuf, sem, *, n_tiles, final_norm):
    bi = pl.program_id(0)
    i = pl.program_id(1)
    ic = jnp.minimum(i, n_tiles - 1)

    @pl.when(jnp.logical_and(bi == 0, i == 0))
    def _():
        yc_scr[...] = jnp.zeros(yc_scr.shape, BF16)

    tt = _suffix_matrix()
    has_prev = ic > 0
    wins = []
    for c in range(NQ):
        kblks, vblks, exists = [], [], []
        for j in range(ATTN_WINDOW):
            rel = c - (ATTN_WINDOW - 1) + j
            kr, vr, r = (kc_ref, vc_ref, rel) if rel >= 0 else (kp_ref, vp_ref, NQ + rel)
            kblks.append(kr[r * QB:(r + 1) * QB, :])
            vblks.append(vr[r * QB:(r + 1) * QB, :])
            exists.append(None if rel >= 0 else has_prev)
        wins.append(_Window(q_ref[c * QB:(c + 1) * QB, :], kblks, vblks, exists, tt))

    ns_all = [w.scores() for w in wins]
    x1 = (x_ref[...]
          + jnp.dot(_after(yl_ref[...], ns_all), wol_ref[...], preferred_element_type=F32)
          + jnp.dot(yc_scr[...], woc_ref[...], preferred_element_type=F32))
    ll_all = [w.suffix_sums() for w in wins]
    h = _after(_rms(x1) * gf_ref[...], ll_all).astype(BF16)
    a = jnp.maximum(jnp.dot(h, wup_ref[...], preferred_element_type=F32), 0.0)
    for c, w in enumerate(wins):
        carry_scr[c] = w.weights()
    pv_all = [w.values() for w in wins]
    for c in range(NQ):
        acc_scr[c] = pv_all[c]
    x2 = x1 + jnp.dot(_after(a * a, pv_all).astype(BF16), wdn_ref[...], preferred_element_type=F32)
    if final_norm:
        x2 = _rms(x2) * gfin_ref[...]
    o_ref[...] = x2

    def fetch(src, dst, slot, kb):
        cp = pltpu.make_async_copy(src.at[bi, pl.ds(pl.multiple_of(kb * QB, QB), QB), :], dst, sem.at[slot])
        cp.start()
        return cp

    def older_blocks(c, _):
        qs = _stack_heads(q_ref[pl.ds(pl.multiple_of(c * QB, QB), QB), :])

        def cond(st):
            kb, live = st
            return jnp.logical_and(kb >= 0, live > -STICK_DEAD)

        def body(st):
            kb, _ = st
            ck = fetch(k_hbm, kbuf, 0, kb)
            cv = fetch(v_hbm, vbuf, 1, kb)
            ck.wait()
            cv.wait()
            carry = carry_scr[c]
            pv, tot = _attn_block(qs, kbuf[...], vbuf[...], tt, carry)
            acc_scr[c] = acc_scr[c] + pv
            carry_scr[c] = carry + tot
            return kb - 1, jnp.max(carry + tot)

        lax.while_loop(cond, body, (ic * NQ + c - ATTN_WINDOW, jnp.max(carry_scr[c])))
        return 0

    lax.fori_loop(0, NQ, older_blocks, 0)

    for c in range(NQ):
        yc_scr[c * QB:(c + 1) * QB, :] = (_rms(acc_scr[c]) * gc_ref[...]).astype(BF16)


def _attn_ffn(x, y_loc, q, k, v, wo_loc, wo_att, g_ffn, w_up, w_dn, g_fin, g_att, final_norm):
    b, s, d = x.shape
    g = D_GROUP
    nt = s // TM
    lag = lambda bi, i: (bi, jnp.maximum(i - 1, 0), 0)
    cur = lambda bi, i: (bi, jnp.minimum(i, nt - 1), 0)
    prev = lambda bi, i: (bi, jnp.maximum(jnp.minimum(i, nt - 1) - 1, 0), 0)
    const = lambda bi, i: (0, 0)
    resident = functools.partial(pl.BlockSpec, index_map=const, pipeline_mode=pl.Buffered(1))
    return pl.pallas_call(
        functools.partial(_attn_ffn_kernel, n_tiles=nt, final_norm=final_norm),
        grid=(b, nt + 1),
        in_specs=[
            pl.BlockSpec((None, TM, d), lag),
            pl.BlockSpec((None, TM, 3 * g), lag),
            pl.BlockSpec((None, TM, g), cur),
            pl.BlockSpec((None, TM, g), cur),
            pl.BlockSpec((None, TM, g), cur),
            pl.BlockSpec((None, TM, g), prev),
            pl.BlockSpec((None, TM, g), prev),
            pl.BlockSpec(memory_space=pl.ANY),
            pl.BlockSpec(memory_space=pl.ANY),
            resident(wo_loc.shape),
            resident(wo_att.shape),
            pl.BlockSpec((1, d), const),
            resident(w_up.shape),
            resident(w_dn.shape),
            pl.BlockSpec((1, d), const),
            pl.BlockSpec((1, g), const),
        ],
        out_specs=pl.BlockSpec((None, TM, d), lag),
        out_shape=jax.ShapeDtypeStruct((b, s, d), F32),
        scratch_shapes=[
            pltpu.VMEM((TM, g), BF16),
            pltpu.VMEM((NQ, N_HEADS * QB, QB), F32),
            pltpu.VMEM((NQ, QB, g), F32),
            pltpu.VMEM((QB, g), BF16),
            pltpu.VMEM((QB, g), BF16),
            pltpu.SemaphoreType.DMA((2,)),
        ],
        compiler_params=pltpu.CompilerParams(
            dimension_semantics=("arbitrary", "arbitrary"), vmem_limit_bytes=56 * MIB),
        name="attn_ffn",
    )(x, y_loc, q, k, v, k, v, k, v, wo_loc, wo_att, g_ffn, w_up, w_dn, g_fin, g_att)


def kernel(x, norm_mix_g, w_in, gmlp_v_g, gmlp_w_s, gmlp_b_s, short_conv_w, conf_conv_w, conf_ln_g, conf_ln_b,
           mix_out_g, w_out, norm_ffn_g, w_up, w_down, final_norm_g):
    depth = w_in.shape[0]
    g = D_GROUP
    for l in range(depth):
        ws_cat = jnp.transpose(gmlp_w_s[l], (1, 0, 2)).reshape(CHUNK, N_HEADS * CHUNK)
        bs_lanes = jnp.repeat(gmlp_b_s[l].T, HEAD_DIM, axis=1)
        go = mix_out_g[l]
        g_loc = jnp.concatenate([go[0:2 * g], go[3 * g:4 * g]])[None, :]
        wo = w_out[l].astype(BF16)
        wo_loc = jnp.concatenate([wo[0:2 * g], wo[3 * g:4 * g]], axis=0)
        y_loc, q, k, v = _proj_local(
            x, norm_mix_g[l][None, :], w_in[l].astype(BF16), gmlp_v_g[l][None, :], ws_cat, bs_lanes,
            short_conv_w[l], conf_conv_w[l], conf_ln_g[l][None, :], conf_ln_b[l][None, :], g_loc)
        x = _attn_ffn(
            x, y_loc, q, k, v, wo_loc, wo[2 * g:3 * g], norm_ffn_g[l][None, :],
            w_up[l].astype(BF16), w_down[l].astype(BF16), final_norm_g[None, :], go[2 * g:3 * g][None, :],
            final_norm=(l == depth - 1))
    return x
```

```python
import functools
import math

import jax
import jax.numpy as jnp
from jax import lax
from jax.experimental import pallas as pl
from jax.experimental.pallas import tpu as pltpu

D_MODEL = 1024
D_GROUP = 256
HEAD_DIM = 64
N_HEADS = D_GROUP // HEAD_DIM
CHUNK = 128
SHORT_W = 3
CONF_W = 31
D_FF = 4 * D_MODEL
D_IN_PROJ = 10 * D_GROUP
EPS = 1e-6

F32 = jnp.float32
BF16 = jnp.bfloat16

SUBLANES = 8
MIB = 1024 * 1024

TM = 512
SUB = 256
QB = CHUNK
NQ = TM // QB
ATTN_WINDOW = 3
SHORT_HALO = SUBLANES
CONF_HALO = 4 * SUBLANES
STICK_DEAD = 110.0


def _rms(x):
    return x * lax.rsqrt(jnp.mean(x * x, axis=-1, keepdims=True) + EPS)


def _gelu_tanh(x):
    c = math.sqrt(2.0 / math.pi)
    return 0.5 * x * (1.0 + jnp.tanh(c * (x + 0.044715 * (x * x * x))))


def _head_of_lane():
    return lax.broadcasted_iota(jnp.int32, (QB, D_GROUP), 1) // HEAD_DIM


def _block_diag(vblk):
    hol = _head_of_lane()
    return [jnp.where(hol == h, vblk, jnp.zeros_like(vblk)) for h in range(N_HEADS)]


def _local_chunk(zr, s0, za, zb, pbuf, hbuf, wcat, vg_ref, bs_ref, wsc_ref, wcc_ref, lng_ref, lnb_ref):
    g = D_GROUP
    r0 = s0 + zr
    ga = _gelu_tanh(za[zr:zr + CHUNK, :])
    u = ga[:, 0:g]
    v = (_rms(ga[:, g:2 * g]) * vg_ref[...]).astype(BF16)
    f = jnp.dot(wcat, jnp.concatenate(_block_diag(v), axis=0), preferred_element_type=F32) + bs_ref[...]
    ya = u * f
    conv_b = wsc_ref[SHORT_W - 1:SHORT_W, :] * pbuf[SHORT_HALO + r0:SHORT_HALO + r0 + CHUNK, :]
    for kk in range(SHORT_W - 1):
        off = SHORT_HALO + r0 - (SHORT_W - 1) + kk
        conv_b = conv_b + wsc_ref[kk:kk + 1, :] * pbuf[off:off + CHUNK, :]
    yb = zb[zr:zr + CHUNK, 0:g] * conv_b
    base = CONF_HALO + r0 - (CONF_W - 1)
    conv_d = None
    for res in range(SUBLANES):
        part = None
        for kk in range(CONF_W):
            if (base + kk) % SUBLANES != res:
                continue
            off = base + kk - res
            term = wcc_ref[kk:kk + 1, :] * hbuf[off:off + CHUNK + SUBLANES, :]
            part = term if part is None else part + term
        if part is None:
            continue
        shifted = part[res:res + CHUNK, :]
        conv_d = shifted if conv_d is None else conv_d + shifted
    xc = conv_d - jnp.mean(conv_d, axis=-1, keepdims=True)
    var = jnp.mean(xc * xc, axis=-1, keepdims=True)
    ln = xc * lax.rsqrt(var + EPS) * lng_ref[...] + lnb_ref[...]
    yd = ln * jax.nn.sigmoid(ln)
    return ya, yb, yd


def _proj_local_kernel(x_ref, gin_ref, win_ref, vg_ref, ws_ref, bs_ref, wsc_ref, wcc_ref, lng_ref, lnb_ref,
                       go_ref, yl_ref, q_ref, k_ref, v_ref, pbuf, hbuf):
    g = D_GROUP
    t = TM

    @pl.when(pl.program_id(1) == 0)
    def _():
        pbuf[0:SHORT_HALO, :] = jnp.zeros((SHORT_HALO, g), F32)
        hbuf[...] = jnp.zeros(hbuf.shape, F32)

    rr = lax.broadcasted_iota(jnp.int32, (CHUNK, N_HEADS * CHUNK), 0)
    cc = lax.broadcasted_iota(jnp.int32, (CHUNK, N_HEADS * CHUNK), 1)
    wcat = jnp.where((cc % CHUNK) <= rr, ws_ref[...], 0.0).astype(BF16)

    for sub in range(TM // SUB):
        s0 = sub * SUB
        h = (_rms(x_ref[s0:s0 + SUB, :]) * gin_ref[...]).astype(BF16)

        def proj(lo, hi):
            return jnp.dot(h, win_ref[:, lo:hi], preferred_element_type=F32)

        q_ref[s0:s0 + SUB, :] = (proj(5 * g, 6 * g) * -(HEAD_DIM ** -0.5)).astype(BF16)
        k_ref[s0:s0 + SUB, :] = proj(6 * g, 7 * g).astype(BF16)
        v_ref[s0:s0 + SUB, :] = proj(7 * g, 8 * g).astype(BF16)
        za = proj(0, 2 * g)
        zb = proj(2 * g, 5 * g)
        zd = proj(8 * g, 10 * g)

        pbuf[SHORT_HALO + s0:SHORT_HALO + s0 + SUB, :] = zb[:, g:2 * g] * zb[:, 2 * g:3 * g]
        hbuf[CONF_HALO + s0:CONF_HALO + s0 + SUB, :] = zd[:, 0:g] * jax.nn.sigmoid(zd[:, g:2 * g])

        for cs in range(SUB // CHUNK):
            r0 = s0 + cs * CHUNK
            ya, yb, yd = _local_chunk(cs * CHUNK, s0, za, zb, pbuf, hbuf, wcat,
                                      vg_ref, bs_ref, wsc_ref, wcc_ref, lng_ref, lnb_ref)
            yl_ref[r0:r0 + CHUNK, 0:g] = (_rms(ya) * go_ref[:, 0:g]).astype(BF16)
            yl_ref[r0:r0 + CHUNK, g:2 * g] = (_rms(yb) * go_ref[:, g:2 * g]).astype(BF16)
            yl_ref[r0:r0 + CHUNK, 2 * g:3 * g] = (_rms(yd) * go_ref[:, 2 * g:3 * g]).astype(BF16)

    pbuf[0:SHORT_HALO, :] = pbuf[t:t + SHORT_HALO, :]
    hbuf[0:CONF_HALO, :] = hbuf[t:t + CONF_HALO, :]


def _proj_local(x, g_in, w_in, vg, ws_cat, bs_lanes, w_short, w_conf, ln_g, ln_b, g_loc):
    b, s, d = x.shape
    g = D_GROUP
    tile = lambda bi, i: (bi, i, 0)
    const = lambda bi, i: (0, 0)
    return pl.pallas_call(
        _proj_local_kernel,
        grid=(b, s // TM),
        in_specs=[
            pl.BlockSpec((None, TM, d), tile),
            pl.BlockSpec((1, d), const),
            pl.BlockSpec((d, D_IN_PROJ), const, pipeline_mode=pl.Buffered(1)),
            pl.BlockSpec((1, g), const),
            pl.BlockSpec((CHUNK, N_HEADS * CHUNK), const),
            pl.BlockSpec((CHUNK, g), const),
            pl.BlockSpec((SHORT_W, g), const),
            pl.BlockSpec((CONF_W, g), const),
            pl.BlockSpec((1, g), const),
            pl.BlockSpec((1, g), const),
            pl.BlockSpec((1, 3 * g), const),
        ],
        out_specs=[
            pl.BlockSpec((None, TM, 3 * g), tile),
            pl.BlockSpec((None, TM, g), tile),
            pl.BlockSpec((None, TM, g), tile),
            pl.BlockSpec((None, TM, g), tile),
        ],
        out_shape=[
            jax.ShapeDtypeStruct((b, s, 3 * g), BF16),
            jax.ShapeDtypeStruct((b, s, g), BF16),
            jax.ShapeDtypeStruct((b, s, g), BF16),
            jax.ShapeDtypeStruct((b, s, g), BF16),
        ],
        scratch_shapes=[
            pltpu.VMEM((SHORT_HALO + TM, g), F32),
            pltpu.VMEM((CONF_HALO + TM + SUBLANES, g), F32),
        ],
        compiler_params=pltpu.CompilerParams(
            dimension_semantics=("arbitrary", "arbitrary"), vmem_limit_bytes=40 * MIB),
        name="proj_local",
    )(x, g_in, w_in, vg, ws_cat, bs_lanes, w_short, w_conf, ln_g, ln_b, g_loc)


def _stack_heads(q):
    return jnp.concatenate(_block_diag(q), axis=0)


def _suffix_matrix():
    rr = lax.broadcasted_iota(jnp.int32, (2 * QB, 2 * QB), 0) % QB
    cc = lax.broadcasted_iota(jnp.int32, (2 * QB, 2 * QB), 1)
    return jnp.where((cc >= QB) | (rr > cc), 1.0, 0.0).astype(BF16)


def _neg_softplus(ns):
    return jnp.minimum(ns, 0.0) - jnp.log(1.0 + jnp.exp2(jnp.abs(ns) * (-math.log2(math.e))))


def _split(lom):
    hi = lom.astype(BF16)
    lo = (lom - hi.astype(F32)).astype(BF16)
    return jnp.concatenate([hi, lo], axis=1)


def _heads_to_lanes(wb):
    return [wb[h * QB:(h + 1) * QB, :] for h in range(N_HEADS)]


class _Window:
    def __init__(self, q, kblks, vblks, exists, tt):
        self.q, self.kblks, self.vblks, self.exists, self.tt = q, kblks, vblks, exists, tt
        self.nw = len(kblks)
        nh = N_HEADS
        qrow = lax.broadcasted_iota(jnp.int32, (nh * QB, QB), 0) % QB
        kcol = lax.broadcasted_iota(jnp.int32, (nh * QB, QB), 1)
        self.causal = kcol < qrow

    def _mask(self, j, val):
        if j == self.nw - 1:
            return jnp.where(self.causal, val, 0.0)
        return val if self.exists[j] is None else jnp.where(self.exists[j], val, 0.0)

    def scores(self):
        self.ns = lax.dot_general(_stack_heads(self.q), jnp.concatenate(self.kblks, axis=0),
                                  (((1,), (1,)), ((), ())), preferred_element_type=F32)
        return self.ns

    def suffix_sums(self):
        nw = self.nw
        self.ns_j = [self.ns[:, j * QB:(j + 1) * QB] for j in range(nw)]
        self.lom_j = [self._mask(j, _neg_softplus(self.ns_j[j])) for j in range(nw)]
        self.ll = jnp.dot(jnp.concatenate([_split(l) for l in self.lom_j], axis=0), self.tt,
                          preferred_element_type=F32)
        return self.ll

    def weights(self):
        nw, nh = self.nw, N_HEADS
        later = None
        wcols = [None] * nw
        for j in reversed(range(nw)):
            rows = slice(j * nh * QB, (j + 1) * nh * QB)
            logw = (self.lom_j[j] - self.ns_j[j]) + self.ll[rows, 0:QB]
            if later is not None:
                logw = logw + later
            wcols[j] = _heads_to_lanes(self._mask(j, jnp.exp(logw)).astype(BF16))
            tot = self.ll[rows, QB:2 * QB]
            later = tot if later is None else later + tot
        self.wcat = jnp.concatenate([c for cols in wcols for c in cols], axis=1)
        return later

    def values(self):
        vbd = jnp.concatenate([m for vb in self.vblks for m in _block_diag(vb)], axis=0)
        return jnp.dot(self.wcat, vbd, preferred_element_type=F32)


def _after(x, vals):
    z = sum(jnp.minimum(jnp.maximum(v[0:1, 0:CHUNK], 0.0), 0.0) for v in vals)
    return jnp.concatenate([x[:, 0:CHUNK] + z.astype(x.dtype), x[:, CHUNK:]], axis=1)


def _attn_block(qs, kblk, vblk, tt, carry):
    nsb = lax.dot_general(qs, kblk, (((1,), (1,)), ((), ())), preferred_element_type=F32)
    lom = _neg_softplus(nsb)
    llb = jnp.dot(_split(lom), tt, preferred_element_type=F32)
    w = jnp.exp((lom - nsb) + llb[:, 0:QB] + carry)
    wcat = jnp.concatenate(_heads_to_lanes(w.astype(BF16)), axis=1)
    vbd = jnp.concatenate(_block_diag(vblk), axis=0)
    return jnp.dot(wcat, vbd, preferred_element_type=F32), llb[:, QB:2 * QB]


def _attn_ffn_kernel(x_ref, yl_ref, q_ref, kc_ref, vc_ref, kp_ref, vp_ref, k_hbm, v_hbm,
                     wol_ref, woc_ref, gf_ref, wup_ref, wdn_ref, gfin_ref, gc_ref,
                     o_ref, yc_scr, carry_scr, acc_scr, kbuf, vbuf, sem, *, n_tiles, final_norm):
    bi = pl.program_id(0)
    i = pl.program_id(1)
    ic = jnp.minimum(i, n_tiles - 1)

    @pl.when(jnp.logical_and(bi == 0, i == 0))
    def _():
        yc_scr[...] = jnp.zeros(yc_scr.shape, BF16)

    tt = _suffix_matrix()
    has_prev = ic > 0
    wins = []
    for c in range(NQ):
        kblks, vblks, exists = [], [], []
        for j in range(ATTN_WINDOW):
            rel = c - (ATTN_WINDOW - 1) + j
            kr, vr, r = (kc_ref, vc_ref, rel) if rel >= 0 else (kp_ref, vp_ref, NQ + rel)
            kblks.append(kr[r * QB:(r + 1) * QB, :])
            vblks.append(vr[r * QB:(r + 1) * QB, :])
            exists.append(None if rel >= 0 else has_prev)
        wins.append(_Window(q_ref[c * QB:(c + 1) * QB, :], kblks, vblks, exists, tt))

    ns_all = [w.scores() for w in wins]
    x1 = (x_ref[...]
          + jnp.dot(_after(yl_ref[...], ns_all), wol_ref[...], preferred_element_type=F32)
          + jnp.dot(yc_scr[...], woc_ref[...], preferred_element_type=F32))
    ll_all = [w.suffix_sums() for w in wins]
    h = _after(_rms(x1) * gf_ref[...], ll_all).astype(BF16)
    a = jnp.maximum(jnp.dot(h, wup_ref[...], preferred_element_type=F32), 0.0)
    carries = [w.weights() for w in wins]
    for c in range(NQ):
        carry_scr[c] = carries[c]
    live_any = jnp.max(functools.reduce(jnp.maximum, carries))
    pv_all = [w.values() for w in wins]
    for c in range(NQ):
        acc_scr[c] = pv_all[c]
        yc_scr[c * QB:(c + 1) * QB, :] = (_rms(pv_all[c]) * gc_ref[...]).astype(BF16)
    x2 = x1 + jnp.dot(_after(a * a, pv_all).astype(BF16), wdn_ref[...], preferred_element_type=F32)
    if final_norm:
        x2 = _rms(x2) * gfin_ref[...]
    o_ref[...] = x2

    def fetch(src, dst, slot, kb):
        cp = pltpu.make_async_copy(src.at[bi, pl.ds(pl.multiple_of(kb * QB, QB), QB), :], dst, sem.at[slot])
        cp.start()
        return cp

    def older_blocks(c, _):
        qs = _stack_heads(q_ref[pl.ds(pl.multiple_of(c * QB, QB), QB), :])

        def cond(st):
            kb, live = st
            return jnp.logical_and(kb >= 0, live > -STICK_DEAD)

        def body(st):
            kb, _ = st
            ck = fetch(k_hbm, kbuf, 0, kb)
            cv = fetch(v_hbm, vbuf, 1, kb)
            ck.wait()
            cv.wait()
            carry = carry_scr[c]
            pv, tot = _attn_block(qs, kbuf[...], vbuf[...], tt, carry)
            acc_scr[c] = acc_scr[c] + pv
            carry_scr[c] = carry + tot
            return kb - 1, jnp.max(carry + tot)

        lax.while_loop(cond, body, (ic * NQ + c - ATTN_WINDOW, jnp.max(carry_scr[c])))
        return 0

    @pl.when(live_any > -STICK_DEAD)
    def _():
        lax.fori_loop(0, NQ, older_blocks, 0)
        for c in range(NQ):
            yc_scr[c * QB:(c + 1) * QB, :] = (_rms(acc_scr[c]) * gc_ref[...]).astype(BF16)


def _attn_ffn(x, y_loc, q, k, v, wo_loc, wo_att, g_ffn, w_up, w_dn, g_fin, g_att, final_norm):
    b, s, d = x.shape
    g = D_GROUP
    nt = s // TM
    lag = lambda bi, i: (bi, jnp.maximum(i - 1, 0), 0)
    cur = lambda bi, i: (bi, jnp.minimum(i, nt - 1), 0)
    prev = lambda bi, i: (bi, jnp.maximum(jnp.minimum(i, nt - 1) - 1, 0), 0)
    const = lambda bi, i: (0, 0)
    resident = functools.partial(pl.BlockSpec, index_map=const, pipeline_mode=pl.Buffered(1))
    return pl.pallas_call(
        functools.partial(_attn_ffn_kernel, n_tiles=nt, final_norm=final_norm),
        grid=(b, nt + 1),
        in_specs=[
            pl.BlockSpec((None, TM, d), lag),
            pl.BlockSpec((None, TM, 3 * g), lag),
            pl.BlockSpec((None, TM, g), cur),
            pl.BlockSpec((None, TM, g), cur),
            pl.BlockSpec((None, TM, g), cur),
            pl.BlockSpec((None, TM, g), prev),
            pl.BlockSpec((None, TM, g), prev),
            pl.BlockSpec(memory_space=pl.ANY),
            pl.BlockSpec(memory_space=pl.ANY),
            resident(wo_loc.shape),
            resident(wo_att.shape),
            pl.BlockSpec((1, d), const),
            resident(w_up.shape),
            resident(w_dn.shape),
            pl.BlockSpec((1, d), const),
            pl.BlockSpec((1, g), const),
        ],
        out_specs=pl.BlockSpec((None, TM, d), lag),
        out_shape=jax.ShapeDtypeStruct((b, s, d), F32),
        scratch_shapes=[
            pltpu.VMEM((TM, g), BF16),
            pltpu.VMEM((NQ, N_HEADS * QB, QB), F32),
            pltpu.VMEM((NQ, QB, g), F32),
            pltpu.VMEM((QB, g), BF16),
            pltpu.VMEM((QB, g), BF16),
            pltpu.SemaphoreType.DMA((2,)),
        ],
        compiler_params=pltpu.CompilerParams(
            dimension_semantics=("arbitrary", "arbitrary"), vmem_limit_bytes=56 * MIB),
        name="attn_ffn",
    )(x, y_loc, q, k, v, k, v, k, v, wo_loc, wo_att, g_ffn, w_up, w_dn, g_fin, g_att)


def kernel(x, norm_mix_g, w_in, gmlp_v_g, gmlp_w_s, gmlp_b_s, short_conv_w, conf_conv_w, conf_ln_g, conf_ln_b,
           mix_out_g, w_out, norm_ffn_g, w_up, w_down, final_norm_g):
    depth = w_in.shape[0]
    g = D_GROUP
    for l in range(depth):
        ws_cat = jnp.transpose(gmlp_w_s[l], (1, 0, 2)).reshape(CHUNK, N_HEADS * CHUNK)
        bs_lanes = jnp.repeat(gmlp_b_s[l].T, HEAD_DIM, axis=1)
        go = mix_out_g[l]
        g_loc = jnp.concatenate([go[0:2 * g], go[3 * g:4 * g]])[None, :]
        wo = w_out[l].astype(BF16)
        wo_loc = jnp.concatenate([wo[0:2 * g], wo[3 * g:4 * g]], axis=0)
        y_loc, q, k, v = _proj_local(
            x, norm_mix_g[l][None, :], w_in[l].astype(BF16), gmlp_v_g[l][None, :], ws_cat, bs_lanes,
            short_conv_w[l], conf_conv_w[l], conf_ln_g[l][None, :], conf_ln_b[l][None, :], g_loc)
        x = _attn_ffn(
            x, y_loc, q, k, v, wo_loc, wo[2 * g:3 * g], norm_ffn_g[l][None, :],
            w_up[l].astype(BF16), w_down[l].astype(BF16), final_norm_g[None, :], go[2 * g:3 * g][None, :],
            final_norm=(l == depth - 1))
    return x
```

```python
import functools
import math

import jax
import jax.numpy as jnp
from jax import lax
from jax.experimental import pallas as pl
from jax.experimental.pallas import tpu as pltpu

D_MODEL = 1024
D_GROUP = 256
HEAD_DIM = 64
N_HEADS = D_GROUP // HEAD_DIM
CHUNK = 128
SHORT_W = 3
CONF_W = 31
D_FF = 4 * D_MODEL
D_IN_PROJ = 10 * D_GROUP
EPS = 1e-6

F32 = jnp.float32
BF16 = jnp.bfloat16

SUBLANES = 8
MIB = 1024 * 1024

TM = 512
TMP = 1024
SUB = 256
QB = CHUNK
NQ = TM // QB
ATTN_WINDOW = 3
SHORT_HALO = SUBLANES
CONF_HALO = 4 * SUBLANES
STICK_DEAD = 110.0


def _rms(x):
    return x * lax.rsqrt(jnp.mean(x * x, axis=-1, keepdims=True) + EPS)


def _gelu_tanh(x):
    c = math.sqrt(2.0 / math.pi)
    return 0.5 * x * (1.0 + jnp.tanh(c * (x + 0.044715 * (x * x * x))))


def _head_of_lane():
    return lax.broadcasted_iota(jnp.int32, (QB, D_GROUP), 1) // HEAD_DIM


def _block_diag(vblk):
    hol = _head_of_lane()
    return [jnp.where(hol == h, vblk, jnp.zeros_like(vblk)) for h in range(N_HEADS)]


def _local_chunk(zr, s0, za, zb, pbuf, hbuf, wcat, vg_ref, bs_ref, wsc_ref, wcc_ref, lng_ref, lnb_ref):
    g = D_GROUP
    r0 = s0 + zr
    ga = _gelu_tanh(za[zr:zr + CHUNK, :])
    u = ga[:, 0:g]
    v = (_rms(ga[:, g:2 * g]) * vg_ref[...]).astype(BF16)
    f = jnp.dot(wcat, jnp.concatenate(_block_diag(v), axis=0), preferred_element_type=F32) + bs_ref[...]
    ya = u * f
    conv_b = wsc_ref[SHORT_W - 1:SHORT_W, :] * pbuf[SHORT_HALO + r0:SHORT_HALO + r0 + CHUNK, :]
    for kk in range(SHORT_W - 1):
        off = SHORT_HALO + r0 - (SHORT_W - 1) + kk
        conv_b = conv_b + wsc_ref[kk:kk + 1, :] * pbuf[off:off + CHUNK, :]
    yb = zb[zr:zr + CHUNK, 0:g] * conv_b
    base = CONF_HALO + r0 - (CONF_W - 1)
    conv_d = None
    for res in range(SUBLANES):
        part = None
        for kk in range(CONF_W):
            if (base + kk) % SUBLANES != res:
                continue
            off = base + kk - res
            term = wcc_ref[kk:kk + 1, :] * hbuf[off:off + CHUNK + SUBLANES, :]
            part = term if part is None else part + term
        if part is None:
            continue
        shifted = part[res:res + CHUNK, :]
        conv_d = shifted if conv_d is None else conv_d + shifted
    xc = conv_d - jnp.mean(conv_d, axis=-1, keepdims=True)
    var = jnp.mean(xc * xc, axis=-1, keepdims=True)
    ln = xc * lax.rsqrt(var + EPS) * lng_ref[...] + lnb_ref[...]
    yd = ln * jax.nn.sigmoid(ln)
    return ya, yb, yd


def _proj_local_kernel(x_ref, gin_ref, win_ref, vg_ref, ws_ref, bs_ref, wsc_ref, wcc_ref, lng_ref, lnb_ref,
                       go_ref, yl_ref, q_ref, k_ref, v_ref, pbuf, hbuf):
    g = D_GROUP
    t = TMP

    @pl.when(pl.program_id(1) == 0)
    def _():
        pbuf[0:SHORT_HALO, :] = jnp.zeros((SHORT_HALO, g), F32)
        hbuf[...] = jnp.zeros(hbuf.shape, F32)

    rr = lax.broadcasted_iota(jnp.int32, (CHUNK, N_HEADS * CHUNK), 0)
    cc = lax.broadcasted_iota(jnp.int32, (CHUNK, N_HEADS * CHUNK), 1)
    wcat = jnp.where((cc % CHUNK) <= rr, ws_ref[...], 0.0).astype(BF16)

    for sub in range(TMP // SUB):
        s0 = sub * SUB
        h = (_rms(x_ref[s0:s0 + SUB, :]) * gin_ref[...]).astype(BF16)

        def proj(lo, hi):
            return jnp.dot(h, win_ref[:, lo:hi], preferred_element_type=F32)

        q_ref[s0:s0 + SUB, :] = (proj(5 * g, 6 * g) * -(HEAD_DIM ** -0.5)).astype(BF16)
        k_ref[s0:s0 + SUB, :] = proj(6 * g, 7 * g).astype(BF16)
        v_ref[s0:s0 + SUB, :] = proj(7 * g, 8 * g).astype(BF16)
        za = proj(0, 2 * g)
        zb = proj(2 * g, 5 * g)
        zd = proj(8 * g, 10 * g)

        pbuf[SHORT_HALO + s0:SHORT_HALO + s0 + SUB, :] = zb[:, g:2 * g] * zb[:, 2 * g:3 * g]
        hbuf[CONF_HALO + s0:CONF_HALO + s0 + SUB, :] = zd[:, 0:g] * jax.nn.sigmoid(zd[:, g:2 * g])

        for cs in range(SUB // CHUNK):
            r0 = s0 + cs * CHUNK
            ya, yb, yd = _local_chunk(cs * CHUNK, s0, za, zb, pbuf, hbuf, wcat,
                                      vg_ref, bs_ref, wsc_ref, wcc_ref, lng_ref, lnb_ref)
            yl_ref[r0:r0 + CHUNK, 0:g] = (_rms(ya) * go_ref[:, 0:g]).astype(BF16)
            yl_ref[r0:r0 + CHUNK, g:2 * g] = (_rms(yb) * go_ref[:, g:2 * g]).astype(BF16)
            yl_ref[r0:r0 + CHUNK, 2 * g:3 * g] = (_rms(yd) * go_ref[:, 2 * g:3 * g]).astype(BF16)

    pbuf[0:SHORT_HALO, :] = pbuf[t:t + SHORT_HALO, :]
    hbuf[0:CONF_HALO, :] = hbuf[t:t + CONF_HALO, :]


def _proj_local(l, x, g_in, w_in, vg, ws_cat, bs_lanes, w_short, w_conf, ln_g, ln_b, g_loc):
    b, s, d = x.shape
    g = D_GROUP
    tile = lambda bi, i: (bi, i, 0)
    const = lambda bi, i: (0, 0)
    return pl.pallas_call(
        _proj_local_kernel,
        grid=(b, s // TMP),
        in_specs=[
            pl.BlockSpec((None, TMP, d), tile),
            pl.BlockSpec((1, d), const),
            pl.BlockSpec((None, d, D_IN_PROJ), lambda bi, i: (l, 0, 0), pipeline_mode=pl.Buffered(1)),
            pl.BlockSpec((1, g), const),
            pl.BlockSpec((CHUNK, N_HEADS * CHUNK), const),
            pl.BlockSpec((CHUNK, g), const),
            pl.BlockSpec((SHORT_W, g), const),
            pl.BlockSpec((CONF_W, g), const),
            pl.BlockSpec((1, g), const),
            pl.BlockSpec((1, g), const),
            pl.BlockSpec((1, 3 * g), const),
        ],
        out_specs=[
            pl.BlockSpec((None, TMP, 3 * g), tile),
            pl.BlockSpec((None, TMP, g), tile),
            pl.BlockSpec((None, TMP, g), tile),
            pl.BlockSpec((None, TMP, g), tile),
        ],
        out_shape=[
            jax.ShapeDtypeStruct((b, s, 3 * g), BF16),
            jax.ShapeDtypeStruct((b, s, g), BF16),
            jax.ShapeDtypeStruct((b, s, g), BF16),
            jax.ShapeDtypeStruct((b, s, g), BF16),
        ],
        scratch_shapes=[
            pltpu.VMEM((SHORT_HALO + TMP, g), F32),
            pltpu.VMEM((CONF_HALO + TMP + SUBLANES, g), F32),
        ],
        compiler_params=pltpu.CompilerParams(
            dimension_semantics=("arbitrary", "arbitrary"), vmem_limit_bytes=40 * MIB),
        name="proj_local",
    )(x, g_in, w_in, vg, ws_cat, bs_lanes, w_short, w_conf, ln_g, ln_b, g_loc)


def _stack_heads(q):
    return jnp.concatenate(_block_diag(q), axis=0)


def _suffix_matrix():
    rr = lax.broadcasted_iota(jnp.int32, (2 * QB, 2 * QB), 0) % QB
    cc = lax.broadcasted_iota(jnp.int32, (2 * QB, 2 * QB), 1)
    return jnp.where((cc >= QB) | (rr > cc), 1.0, 0.0).astype(BF16)


def _neg_softplus(ns):
    return jnp.minimum(ns, 0.0) - jnp.log(1.0 + jnp.exp2(jnp.abs(ns) * (-math.log2(math.e))))


def _split(lom):
    hi = lom.astype(BF16)
    lo = (lom - hi.astype(F32)).astype(BF16)
    return jnp.concatenate([hi, lo], axis=1)


def _heads_to_lanes(wb):
    return [wb[h * QB:(h + 1) * QB, :] for h in range(N_HEADS)]


class _Window:
    def __init__(self, q, kblks, vblks, exists, tt):
        self.q, self.kblks, self.vblks, self.exists, self.tt = q, kblks, vblks, exists, tt
        self.nw = len(kblks)
        nh = N_HEADS
        qrow = lax.broadcasted_iota(jnp.int32, (nh * QB, QB), 0) % QB
        kcol = lax.broadcasted_iota(jnp.int32, (nh * QB, QB), 1)
        self.causal = kcol < qrow

    def _mask(self, j, val):
        if j == self.nw - 1:
            return jnp.where(self.causal, val, 0.0)
        return val if self.exists[j] is None else jnp.where(self.exists[j], val, 0.0)

    def scores(self):
        self.ns = lax.dot_general(_stack_heads(self.q), jnp.concatenate(self.kblks, axis=0),
                                  (((1,), (1,)), ((), ())), preferred_element_type=F32)
        return self.ns

    def suffix_sums(self):
        nw = self.nw
        self.ns_j = [self.ns[:, j * QB:(j + 1) * QB] for j in range(nw)]
        self.lom_j = [self._mask(j, _neg_softplus(self.ns_j[j])) for j in range(nw)]
        self.ll = jnp.dot(jnp.concatenate([_split(l) for l in self.lom_j], axis=0), self.tt,
                          preferred_element_type=F32)
        return self.ll

    def weights(self):
        nw, nh = self.nw, N_HEADS
        later = None
        wcols = [None] * nw
        for j in reversed(range(nw)):
            rows = slice(j * nh * QB, (j + 1) * nh * QB)
            logw = (self.lom_j[j] - self.ns_j[j]) + self.ll[rows, 0:QB]
            if later is not None:
                logw = logw + later
            wcols[j] = _heads_to_lanes(self._mask(j, jnp.exp(logw)).astype(BF16))
            tot = self.ll[rows, QB:2 * QB]
            later = tot if later is None else later + tot
        self.wcat = jnp.concatenate([c for cols in wcols for c in cols], axis=1)
        return later

    def values(self):
        vbd = jnp.concatenate([m for vb in self.vblks for m in _block_diag(vb)], axis=0)
        return jnp.dot(self.wcat, vbd, preferred_element_type=F32)


def _after(x, vals):
    z = sum(jnp.minimum(jnp.maximum(v[0:1, 0:CHUNK], 0.0), 0.0) for v in vals)
    return jnp.concatenate([x[:, 0:CHUNK] + z.astype(x.dtype), x[:, CHUNK:]], axis=1)


def _attn_block(qs, kblk, vblk, tt, carry):
    nsb = lax.dot_general(qs, kblk, (((1,), (1,)), ((), ())), preferred_element_type=F32)
    lom = _neg_softplus(nsb)
    llb = jnp.dot(_split(lom), tt, preferred_element_type=F32)
    w = jnp.exp((lom - nsb) + llb[:, 0:QB] + carry)
    wcat = jnp.concatenate(_heads_to_lanes(w.astype(BF16)), axis=1)
    vbd = jnp.concatenate(_block_diag(vblk), axis=0)
    return jnp.dot(wcat, vbd, preferred_element_type=F32), llb[:, QB:2 * QB]


def _attn_ffn_kernel(x_ref, yl_ref, q_ref, kc_ref, vc_ref, kp_ref, vp_ref, k_hbm, v_hbm,
                     wo_ref, gf_ref, wup_ref, wdn_ref, gfin_ref, gc_ref,
                     o_ref, yc_scr, carry_scr, acc_scr, kbuf, vbuf, sem, *, n_tiles, final_norm):
    bi = pl.program_id(0)
    i = pl.program_id(1)
    ic = jnp.minimum(i, n_tiles - 1)

    @pl.when(jnp.logical_and(bi == 0, i == 0))
    def _():
        yc_scr[...] = jnp.zeros(yc_scr.shape, BF16)

    tt = _suffix_matrix()
    has_prev = ic > 0
    wins = []
    for c in range(NQ):
        kblks, vblks, exists = [], [], []
        for j in range(ATTN_WINDOW):
            rel = c - (ATTN_WINDOW - 1) + j
            kr, vr, r = (kc_ref, vc_ref, rel) if rel >= 0 else (kp_ref, vp_ref, NQ + rel)
            kblks.append(kr[r * QB:(r + 1) * QB, :])
            vblks.append(vr[r * QB:(r + 1) * QB, :])
            exists.append(None if rel >= 0 else has_prev)
        wins.append(_Window(q_ref[c * QB:(c + 1) * QB, :], kblks, vblks, exists, tt))

    ns_all = [w.scores() for w in wins]
    g = D_GROUP
    yl = _after(yl_ref[...], ns_all)
    y = jnp.concatenate([yl[:, 0:2 * g], yc_scr[...], yl[:, 2 * g:3 * g]], axis=1)
    x1 = x_ref[...] + jnp.dot(y, wo_ref[...], preferred_element_type=F32)
    ll_all = [w.suffix_sums() for w in wins]
    h = _after(_rms(x1) * gf_ref[...], ll_all).astype(BF16)
    a = jnp.maximum(jnp.dot(h, wup_ref[...], preferred_element_type=F32), 0.0)
    carries = [w.weights() for w in wins]
    for c in range(NQ):
        carry_scr[c] = carries[c]
    live_any = jnp.max(functools.reduce(jnp.maximum, carries))
    pv_all = [w.values() for w in wins]
    for c in range(NQ):
        acc_scr[c] = pv_all[c]
        yc_scr[c * QB:(c + 1) * QB, :] = (_rms(pv_all[c]) * gc_ref[...]).astype(BF16)
    x2 = x1 + jnp.dot(_after(a * a, pv_all).astype(BF16), wdn_ref[...], preferred_element_type=F32)
    if final_norm:
        x2 = _rms(x2) * gfin_ref[...]
    o_ref[...] = x2

    def fetch(src, dst, slot, kb):
        cp = pltpu.make_async_copy(src.at[bi, pl.ds(pl.multiple_of(kb * QB, QB), QB), :], dst, sem.at[slot])
        cp.start()
        return cp

    def older_blocks(c, _):
        qs = _stack_heads(q_ref[pl.ds(pl.multiple_of(c * QB, QB), QB), :])

        def cond(st):
            kb, live = st
            return jnp.logical_and(kb >= 0, live > -STICK_DEAD)

        def body(st):
            kb, _ = st
            ck = fetch(k_hbm, kbuf, 0, kb)
            cv = fetch(v_hbm, vbuf, 1, kb)
            ck.wait()
            cv.wait()
            carry = carry_scr[c]
            pv, tot = _attn_block(qs, kbuf[...], vbuf[...], tt, carry)
            acc_scr[c] = acc_scr[c] + pv
            carry_scr[c] = carry + tot
            return kb - 1, jnp.max(carry + tot)

        lax.while_loop(cond, body, (ic * NQ + c - ATTN_WINDOW, jnp.max(carry_scr[c])))
        return 0

    @pl.when(live_any > -STICK_DEAD)
    def _():
        lax.fori_loop(0, NQ, older_blocks, 0)
        for c in range(NQ):
            yc_scr[c * QB:(c + 1) * QB, :] = (_rms(acc_scr[c]) * gc_ref[...]).astype(BF16)


def _attn_ffn(l, x, y_loc, q, k, v, w_out, g_ffn, w_up, w_dn, g_fin, g_att, final_norm):
    b, s, d = x.shape
    g = D_GROUP
    nt = s // TM
    lag = lambda bi, i: (bi, jnp.maximum(i - 1, 0), 0)
    cur = lambda bi, i: (bi, jnp.minimum(i, nt - 1), 0)
    prev = lambda bi, i: (bi, jnp.maximum(jnp.minimum(i, nt - 1) - 1, 0), 0)
    const = lambda bi, i: (0, 0)
    resident = lambda w: pl.BlockSpec((None,) + w.shape[1:], lambda bi, i: (l, 0, 0), pipeline_mode=pl.Buffered(1))
    return pl.pallas_call(
        functools.partial(_attn_ffn_kernel, n_tiles=nt, final_norm=final_norm),
        grid=(b, nt + 1),
        in_specs=[
            pl.BlockSpec((None, TM, d), lag),
            pl.BlockSpec((None, TM, 3 * g), lag),
            pl.BlockSpec((None, TM, g), cur),
            pl.BlockSpec((None, TM, g), cur),
            pl.BlockSpec((None, TM, g), cur),
            pl.BlockSpec((None, TM, g), prev),
            pl.BlockSpec((None, TM, g), prev),
            pl.BlockSpec(memory_space=pl.ANY),
            pl.BlockSpec(memory_space=pl.ANY),
            resident(w_out),
            pl.BlockSpec((1, d), const),
            resident(w_up),
            resident(w_dn),
            pl.BlockSpec((1, d), const),
            pl.BlockSpec((1, g), const),
        ],
        out_specs=pl.BlockSpec((None, TM, d), lag),
        out_shape=jax.ShapeDtypeStruct((b, s, d), F32),
        scratch_shapes=[
            pltpu.VMEM((TM, g), BF16),
            pltpu.VMEM((NQ, N_HEADS * QB, QB), F32),
            pltpu.VMEM((NQ, QB, g), F32),
            pltpu.VMEM((QB, g), BF16),
            pltpu.VMEM((QB, g), BF16),
            pltpu.SemaphoreType.DMA((2,)),
        ],
        compiler_params=pltpu.CompilerParams(
            dimension_semantics=("arbitrary", "arbitrary"), vmem_limit_bytes=56 * MIB),
        name="attn_ffn",
    )(x, y_loc, q, k, v, k, v, k, v, w_out, g_ffn, w_up, w_dn, g_fin, g_att)


def kernel(x, norm_mix_g, w_in, gmlp_v_g, gmlp_w_s, gmlp_b_s, short_conv_w, conf_conv_w, conf_ln_g, conf_ln_b,
           mix_out_g, w_out, norm_ffn_g, w_up, w_down, final_norm_g):
    depth = w_in.shape[0]
    g = D_GROUP
    w_in_b, w_out_b, w_up_b, w_down_b = (w.astype(BF16) for w in (w_in, w_out, w_up, w_down))
    for l in range(depth):
        ws_cat = jnp.transpose(gmlp_w_s[l], (1, 0, 2)).reshape(CHUNK, N_HEADS * CHUNK)
        bs_lanes = jnp.repeat(gmlp_b_s[l].T, HEAD_DIM, axis=1)
        go = mix_out_g[l]
        g_loc = jnp.concatenate([go[0:2 * g], go[3 * g:4 * g]])[None, :]
        y_loc, q, k, v = _proj_local(
            l, x, norm_mix_g[l][None, :], w_in_b, gmlp_v_g[l][None, :], ws_cat, bs_lanes,
            short_conv_w[l], conf_conv_w[l], conf_ln_g[l][None, :], conf_ln_b[l][None, :], g_loc)
        x = _attn_ffn(
            l, x, y_loc, q, k, v, w_out_b, norm_ffn_g[l][None, :], w_up_b, w_down_b,
            final_norm_g[None, :], go[2 * g:3 * g][None, :], final_norm=(l == depth - 1))
    return x
```

```python
import functools
import math

import jax
import jax.numpy as jnp
from jax import lax
from jax.experimental import pallas as pl
from jax.experimental.pallas import tpu as pltpu

D_MODEL = 1024
D_GROUP = 256
HEAD_DIM = 64
N_HEADS = D_GROUP // HEAD_DIM
CHUNK = 128
SHORT_W = 3
CONF_W = 31
D_FF = 4 * D_MODEL
D_IN_PROJ = 10 * D_GROUP
EPS = 1e-6

F32 = jnp.float32
BF16 = jnp.bfloat16

SUBLANES = 8
MIB = 1024 * 1024

TM = 512
TMP = 1024
SUB = 256
QB = CHUNK
NQ = TM // QB
ATTN_WINDOW = 3
SHORT_HALO = SUBLANES
CONF_HALO = 4 * SUBLANES
STICK_DEAD = 110.0


def _rms(x):
    return x * lax.rsqrt(jnp.mean(x * x, axis=-1, keepdims=True) + EPS)


def _gelu_tanh(x):
    c = math.sqrt(2.0 / math.pi)
    return 0.5 * x * (1.0 + jnp.tanh(c * (x + 0.044715 * (x * x * x))))


def _head_of_lane():
    return lax.broadcasted_iota(jnp.int32, (QB, D_GROUP), 1) // HEAD_DIM


def _block_diag(vblk):
    hol = _head_of_lane()
    return [jnp.where(hol == h, vblk, jnp.zeros_like(vblk)) for h in range(N_HEADS)]


def _local_chunk(zr, s0, za, zb, pbuf, hbuf, wcat, vg_ref, bs_ref, wsc_ref, wcc_ref, lng_ref, lnb_ref):
    g = D_GROUP
    r0 = s0 + zr
    ga = _gelu_tanh(za[zr:zr + CHUNK, :])
    u = ga[:, 0:g]
    v = (_rms(ga[:, g:2 * g]) * vg_ref[...]).astype(BF16)
    f = jnp.dot(wcat, jnp.concatenate(_block_diag(v), axis=0), preferred_element_type=F32) + bs_ref[...]
    ya = u * f
    conv_b = wsc_ref[SHORT_W - 1:SHORT_W, :] * pbuf[SHORT_HALO + r0:SHORT_HALO + r0 + CHUNK, :]
    for kk in range(SHORT_W - 1):
        off = SHORT_HALO + r0 - (SHORT_W - 1) + kk
        conv_b = conv_b + wsc_ref[kk:kk + 1, :] * pbuf[off:off + CHUNK, :]
    yb = zb[zr:zr + CHUNK, 0:g] * conv_b
    base = CONF_HALO + r0 - (CONF_W - 1)
    conv_d = None
    for res in range(SUBLANES):
        part = None
        for kk in range(CONF_W):
            if (base + kk) % SUBLANES != res:
                continue
            off = base + kk - res
            term = wcc_ref[kk:kk + 1, :] * hbuf[off:off + CHUNK + SUBLANES, :]
            part = term if part is None else part + term
        if part is None:
            continue
        shifted = part[res:res + CHUNK, :]
        conv_d = shifted if conv_d is None else conv_d + shifted
    xc = conv_d - jnp.mean(conv_d, axis=-1, keepdims=True)
    var = jnp.mean(xc * xc, axis=-1, keepdims=True)
    ln = xc * lax.rsqrt(var + EPS) * lng_ref[...] + lnb_ref[...]
    yd = ln * jax.nn.sigmoid(ln)
    return ya, yb, yd


def _proj_local_kernel(x_ref, gin_ref, win_ref, vg_ref, ws_ref, bs_ref, wsc_ref, wcc_ref, lng_ref, lnb_ref,
                       go_ref, yl_ref, q_ref, k_ref, v_ref, pbuf, hbuf):
    g = D_GROUP
    t = TMP

    @pl.when(pl.program_id(1) == 0)
    def _():
        pbuf[0:SHORT_HALO, :] = jnp.zeros((SHORT_HALO, g), F32)
        hbuf[...] = jnp.zeros(hbuf.shape, F32)

    rr = lax.broadcasted_iota(jnp.int32, (CHUNK, N_HEADS * CHUNK), 0)
    cc = lax.broadcasted_iota(jnp.int32, (CHUNK, N_HEADS * CHUNK), 1)
    wcat = jnp.where((cc % CHUNK) <= rr, ws_ref[...], 0.0).astype(BF16)

    nsub = TMP // SUB

    def proj(sub, lo, hi):
        return jnp.dot(hs[sub], win_ref[:, lo:hi], preferred_element_type=F32)

    hs, zas, zbs = [], [], []
    for sub in range(nsub):
        s0 = sub * SUB
        hs.append((_rms(x_ref[s0:s0 + SUB, :]) * gin_ref[...]).astype(BF16))
        zd = proj(sub, 8 * g, 10 * g)
        hbuf[CONF_HALO + s0:CONF_HALO + s0 + SUB, :] = zd[:, 0:g] * jax.nn.sigmoid(zd[:, g:2 * g])
    for sub in range(nsub):
        s0 = sub * SUB
        zb = proj(sub, 2 * g, 5 * g)
        zbs.append(zb)
        pbuf[SHORT_HALO + s0:SHORT_HALO + s0 + SUB, :] = zb[:, g:2 * g] * zb[:, 2 * g:3 * g]
    for sub in range(nsub):
        zas.append(proj(sub, 0, 2 * g))
    for sub in range(nsub):
        s0 = sub * SUB
        for cs in range(SUB // CHUNK):
            r0 = s0 + cs * CHUNK
            ya, yb, yd = _local_chunk(cs * CHUNK, s0, zas[sub], zbs[sub], pbuf, hbuf, wcat,
                                      vg_ref, bs_ref, wsc_ref, wcc_ref, lng_ref, lnb_ref)
            yl_ref[r0:r0 + CHUNK, 0:g] = (_rms(ya) * go_ref[:, 0:g]).astype(BF16)
            yl_ref[r0:r0 + CHUNK, g:2 * g] = (_rms(yb) * go_ref[:, g:2 * g]).astype(BF16)
            yl_ref[r0:r0 + CHUNK, 2 * g:3 * g] = (_rms(yd) * go_ref[:, 2 * g:3 * g]).astype(BF16)
    for sub in range(nsub):
        s0 = sub * SUB
        qkv = proj(sub, 5 * g, 8 * g)
        q_ref[s0:s0 + SUB, :] = (qkv[:, 0:g] * -(HEAD_DIM ** -0.5)).astype(BF16)
        k_ref[s0:s0 + SUB, :] = qkv[:, g:2 * g].astype(BF16)
        v_ref[s0:s0 + SUB, :] = qkv[:, 2 * g:3 * g].astype(BF16)

    pbuf[0:SHORT_HALO, :] = pbuf[t:t + SHORT_HALO, :]
    hbuf[0:CONF_HALO, :] = hbuf[t:t + CONF_HALO, :]


def _proj_local(l, x, g_in, w_in, vg, ws_cat, bs_lanes, w_short, w_conf, ln_g, ln_b, g_loc):
    b, s, d = x.shape
    g = D_GROUP
    tile = lambda bi, i: (bi, i, 0)
    const = lambda bi, i: (0, 0)
    return pl.pallas_call(
        _proj_local_kernel,
        grid=(b, s // TMP),
        in_specs=[
            pl.BlockSpec((None, TMP, d), tile),
            pl.BlockSpec((1, d), const),
            pl.BlockSpec((None, d, D_IN_PROJ), lambda bi, i: (l, 0, 0), pipeline_mode=pl.Buffered(1)),
            pl.BlockSpec((1, g), const),
            pl.BlockSpec((CHUNK, N_HEADS * CHUNK), const),
            pl.BlockSpec((CHUNK, g), const),
            pl.BlockSpec((SHORT_W, g), const),
            pl.BlockSpec((CONF_W, g), const),
            pl.BlockSpec((1, g), const),
            pl.BlockSpec((1, g), const),
            pl.BlockSpec((1, 3 * g), const),
        ],
        out_specs=[
            pl.BlockSpec((None, TMP, 3 * g), tile),
            pl.BlockSpec((None, TMP, g), tile),
            pl.BlockSpec((None, TMP, g), tile),
            pl.BlockSpec((None, TMP, g), tile),
        ],
        out_shape=[
            jax.ShapeDtypeStruct((b, s, 3 * g), BF16),
            jax.ShapeDtypeStruct((b, s, g), BF16),
            jax.ShapeDtypeStruct((b, s, g), BF16),
            jax.ShapeDtypeStruct((b, s, g), BF16),
        ],
        scratch_shapes=[
            pltpu.VMEM((SHORT_HALO + TMP, g), F32),
            pltpu.VMEM((CONF_HALO + TMP + SUBLANES, g), F32),
        ],
        compiler_params=pltpu.CompilerParams(
            dimension_semantics=("arbitrary", "arbitrary"), vmem_limit_bytes=40 * MIB),
        name="proj_local",
    )(x, g_in, w_in, vg, ws_cat, bs_lanes, w_short, w_conf, ln_g, ln_b, g_loc)


def _stack_heads(q):
    return jnp.concatenate(_block_diag(q), axis=0)


def _suffix_matrix():
    rr = lax.broadcasted_iota(jnp.int32, (2 * QB, 2 * QB), 0) % QB
    cc = lax.broadcasted_iota(jnp.int32, (2 * QB, 2 * QB), 1)
    return jnp.where((cc >= QB) | (rr > cc), 1.0, 0.0).astype(BF16)


def _neg_softplus(ns):
    return jnp.minimum(ns, 0.0) - jnp.log(1.0 + jnp.exp2(jnp.abs(ns) * (-math.log2(math.e))))


def _split(lom):
    hi = lom.astype(BF16)
    lo = (lom - hi.astype(F32)).astype(BF16)
    return jnp.concatenate([hi, lo], axis=1)


def _heads_to_lanes(wb):
    return [wb[h * QB:(h + 1) * QB, :] for h in range(N_HEADS)]


class _Window:
    def __init__(self, q, kblks, vblks, exists, tt):
        self.q, self.kblks, self.vblks, self.exists, self.tt = q, kblks, vblks, exists, tt
        self.nw = len(kblks)
        nh = N_HEADS
        qrow = lax.broadcasted_iota(jnp.int32, (nh * QB, QB), 0) % QB
        kcol = lax.broadcasted_iota(jnp.int32, (nh * QB, QB), 1)
        self.causal = kcol < qrow

    def _mask(self, j, val):
        if j == self.nw - 1:
            return jnp.where(self.causal, val, 0.0)
        return val if self.exists[j] is None else jnp.where(self.exists[j], val, 0.0)

    def scores(self):
        self.ns = lax.dot_general(_stack_heads(self.q), jnp.concatenate(self.kblks, axis=0),
                                  (((1,), (1,)), ((), ())), preferred_element_type=F32)
        return self.ns

    def suffix_sums(self):
        nw = self.nw
        self.ns_j = [self.ns[:, j * QB:(j + 1) * QB] for j in range(nw)]
        self.lom_j = [self._mask(j, _neg_softplus(self.ns_j[j])) for j in range(nw)]
        self.ll = jnp.dot(jnp.concatenate([_split(l) for l in self.lom_j], axis=0), self.tt,
                          preferred_element_type=F32)
        return self.ll

    def weights(self):
        nw, nh = self.nw, N_HEADS
        later = None
        wcols = [None] * nw
        for j in reversed(range(nw)):
            rows = slice(j * nh * QB, (j + 1) * nh * QB)
            logw = (self.lom_j[j] - self.ns_j[j]) + self.ll[rows, 0:QB]
            if later is not None:
                logw = logw + later
            wcols[j] = _heads_to_lanes(self._mask(j, jnp.exp(logw)).astype(BF16))
            tot = self.ll[rows, QB:2 * QB]
            later = tot if later is None else later + tot
        self.wcat = jnp.concatenate([c for cols in wcols for c in cols], axis=1)
        return later

    def values(self):
        vbd = jnp.concatenate([m for vb in self.vblks for m in _block_diag(vb)], axis=0)
        return jnp.dot(self.wcat, vbd, preferred_element_type=F32)


def _after(x, vals):
    z = sum(jnp.minimum(jnp.maximum(v[0:1, 0:CHUNK], 0.0), 0.0) for v in vals)
    return jnp.concatenate([x[:, 0:CHUNK] + z.astype(x.dtype), x[:, CHUNK:]], axis=1)


def _attn_block(qs, kblk, vblk, tt, carry):
    nsb = lax.dot_general(qs, kblk, (((1,), (1,)), ((), ())), preferred_element_type=F32)
    lom = _neg_softplus(nsb)
    llb = jnp.dot(_split(lom), tt, preferred_element_type=F32)
    w = jnp.exp((lom - nsb) + llb[:, 0:QB] + carry)
    wcat = jnp.concatenate(_heads_to_lanes(w.astype(BF16)), axis=1)
    vbd = jnp.concatenate(_block_diag(vblk), axis=0)
    return jnp.dot(wcat, vbd, preferred_element_type=F32), llb[:, QB:2 * QB]


def _attn_ffn_kernel(x_ref, yl_ref, q_ref, kc_ref, vc_ref, kp_ref, vp_ref, k_hbm, v_hbm,
                     wo_ref, gf_ref, wup_ref, wdn_ref, gfin_ref, gc_ref,
                     o_ref, yc_scr, carry_scr, acc_scr, kbuf, vbuf, sem, *, n_tiles, final_norm):
    bi = pl.program_id(0)
    i = pl.program_id(1)
    ic = jnp.minimum(i, n_tiles - 1)

    @pl.when(jnp.logical_and(bi == 0, i == 0))
    def _():
        yc_scr[...] = jnp.zeros(yc_scr.shape, BF16)

    tt = _suffix_matrix()
    has_prev = ic > 0
    wins = []
    for c in range(NQ):
        kblks, vblks, exists = [], [], []
        for j in range(ATTN_WINDOW):
            rel = c - (ATTN_WINDOW - 1) + j
            kr, vr, r = (kc_ref, vc_ref, rel) if rel >= 0 else (kp_ref, vp_ref, NQ + rel)
            kblks.append(kr[r * QB:(r + 1) * QB, :])
            vblks.append(vr[r * QB:(r + 1) * QB, :])
            exists.append(None if rel >= 0 else has_prev)
        wins.append(_Window(q_ref[c * QB:(c + 1) * QB, :], kblks, vblks, exists, tt))

    ns_all = [w.scores() for w in wins]
    g = D_GROUP
    yl = _after(yl_ref[...], ns_all)
    y = jnp.concatenate([yl[:, 0:2 * g], yc_scr[...], yl[:, 2 * g:3 * g]], axis=1)
    x1 = x_ref[...] + jnp.dot(y, wo_ref[...], preferred_element_type=F32)
    ll_all = [w.suffix_sums() for w in wins]
    h = _after(_rms(x1) * gf_ref[...], ll_all).astype(BF16)
    a = jnp.maximum(jnp.dot(h, wup_ref[...], preferred_element_type=F32), 0.0)
    carries = [w.weights() for w in wins]
    for c in range(NQ):
        carry_scr[c] = carries[c]
    live_any = jnp.max(functools.reduce(jnp.maximum, carries))
    pv_all = [w.values() for w in wins]
    for c in range(NQ):
        acc_scr[c] = pv_all[c]
        yc_scr[c * QB:(c + 1) * QB, :] = (_rms(pv_all[c]) * gc_ref[...]).astype(BF16)
    x2 = x1 + jnp.dot(_after(a * a, pv_all).astype(BF16), wdn_ref[...], preferred_element_type=F32)
    if final_norm:
        x2 = _rms(x2) * gfin_ref[...]
    o_ref[...] = x2

    def fetch(src, dst, slot, kb):
        cp = pltpu.make_async_copy(src.at[bi, pl.ds(pl.multiple_of(kb * QB, QB), QB), :], dst, sem.at[slot])
        cp.start()
        return cp

    def older_blocks(c, _):
        qs = _stack_heads(q_ref[pl.ds(pl.multiple_of(c * QB, QB), QB), :])

        def cond(st):
            kb, live = st
            return jnp.logical_and(kb >= 0, live > -STICK_DEAD)

        def body(st):
            kb, _ = st
            ck = fetch(k_hbm, kbuf, 0, kb)
            cv = fetch(v_hbm, vbuf, 1, kb)
            ck.wait()
            cv.wait()
            carry = carry_scr[c]
            pv, tot = _attn_block(qs, kbuf[...], vbuf[...], tt, carry)
            acc_scr[c] = acc_scr[c] + pv
            carry_scr[c] = carry + tot
            return kb - 1, jnp.max(carry + tot)

        lax.while_loop(cond, body, (ic * NQ + c - ATTN_WINDOW, jnp.max(carry_scr[c])))
        return 0

    @pl.when(live_any > -STICK_DEAD)
    def _():
        lax.fori_loop(0, NQ, older_blocks, 0)
        for c in range(NQ):
            yc_scr[c * QB:(c + 1) * QB, :] = (_rms(acc_scr[c]) * gc_ref[...]).astype(BF16)


def _attn_ffn(l, x, y_loc, q, k, v, w_out, g_ffn, w_up, w_dn, g_fin, g_att, final_norm):
    b, s, d = x.shape
    g = D_GROUP
    nt = s // TM
    lag = lambda bi, i: (bi, jnp.maximum(i - 1, 0), 0)
    cur = lambda bi, i: (bi, jnp.minimum(i, nt - 1), 0)
    prev = lambda bi, i: (bi, jnp.maximum(jnp.minimum(i, nt - 1) - 1, 0), 0)
    const = lambda bi, i: (0, 0)
    resident = lambda w: pl.BlockSpec((None,) + w.shape[1:], lambda bi, i: (l, 0, 0), pipeline_mode=pl.Buffered(1))
    return pl.pallas_call(
        functools.partial(_attn_ffn_kernel, n_tiles=nt, final_norm=final_norm),
        grid=(b, nt + 1),
        in_specs=[
            pl.BlockSpec((None, TM, d), lag),
            pl.BlockSpec((None, TM, 3 * g), lag),
            pl.BlockSpec((None, TM, g), cur),
            pl.BlockSpec((None, TM, g), cur),
            pl.BlockSpec((None, TM, g), cur),
            pl.BlockSpec((None, TM, g), prev),
            pl.BlockSpec((None, TM, g), prev),
            pl.BlockSpec(memory_space=pl.ANY),
            pl.BlockSpec(memory_space=pl.ANY),
            resident(w_out),
            pl.BlockSpec((1, d), const),
            resident(w_up),
            resident(w_dn),
            pl.BlockSpec((1, d), const),
            pl.BlockSpec((1, g), const),
        ],
        out_specs=pl.BlockSpec((None, TM, d), lag),
        out_shape=jax.ShapeDtypeStruct((b, s, d), F32),
        scratch_shapes=[
            pltpu.VMEM((TM, g), BF16),
            pltpu.VMEM((NQ, N_HEADS * QB, QB), F32),
            pltpu.VMEM((NQ, QB, g), F32),
            pltpu.VMEM((QB, g), BF16),
            pltpu.VMEM((QB, g), BF16),
            pltpu.SemaphoreType.DMA((2,)),
        ],
        compiler_params=pltpu.CompilerParams(
            dimension_semantics=("arbitrary", "arbitrary"), vmem_limit_bytes=56 * MIB),
        name="attn_ffn",
    )(x, y_loc, q, k, v, k, v, k, v, w_out, g_ffn, w_up, w_dn, g_fin, g_att)


def kernel(x, norm_mix_g, w_in, gmlp_v_g, gmlp_w_s, gmlp_b_s, short_conv_w, conf_conv_w, conf_ln_g, conf_ln_b,
           mix_out_g, w_out, norm_ffn_g, w_up, w_down, final_norm_g):
    depth = w_in.shape[0]
    g = D_GROUP
    w_in_b, w_out_b, w_up_b, w_down_b = (w.astype(BF16) for w in (w_in, w_out, w_up, w_down))
    for l in range(depth):
        ws_cat = jnp.transpose(gmlp_w_s[l], (1, 0, 2)).reshape(CHUNK, N_HEADS * CHUNK)
        bs_lanes = jnp.repeat(gmlp_b_s[l].T, HEAD_DIM, axis=1)
        go = mix_out_g[l]
        g_loc = jnp.concatenate([go[0:2 * g], go[3 * g:4 * g]])[None, :]
        y_loc, q, k, v = _proj_local(
            l, x, norm_mix_g[l][None, :], w_in_b, gmlp_v_g[l][None, :], ws_cat, bs_lanes,
            short_conv_w[l], conf_conv_w[l], conf_ln_g[l][None, :], conf_ln_b[l][None, :], g_loc)
        x = _attn_ffn(
            l, x, y_loc, q, k, v, w_out_b, norm_ffn_g[l][None, :], w_up_b, w_down_b,
            final_norm_g[None, :], go[2 * g:3 * g][None, :], final_norm=(l == depth - 1))
    return x
```

```python
import functools
import math

import jax
import jax.numpy as jnp
from jax import lax
from jax.experimental import pallas as pl
from jax.experimental.pallas import tpu as pltpu

D_MODEL = 1024
D_GROUP = 256
HEAD_DIM = 64
N_HEADS = D_GROUP // HEAD_DIM
CHUNK = 128
SHORT_W = 3
CONF_W = 31
D_FF = 4 * D_MODEL
D_IN_PROJ = 10 * D_GROUP
EPS = 1e-6

F32 = jnp.float32
BF16 = jnp.bfloat16

SUBLANES = 8
MIB = 1024 * 1024

TM = 512
TMP = 1024
SUB = 256
QB = CHUNK
NQ = TM // QB
ATTN_WINDOW = 3
SHORT_HALO = SUBLANES
CONF_HALO = 4 * SUBLANES
STICK_DEAD = 110.0


def _rms(x):
    return x * lax.rsqrt(jnp.mean(x * x, axis=-1, keepdims=True) + EPS)


def _gelu_tanh(x):
    c = math.sqrt(2.0 / math.pi)
    return 0.5 * x * (1.0 + jnp.tanh(c * (x + 0.044715 * (x * x * x))))


def _head_of_lane():
    return lax.broadcasted_iota(jnp.int32, (QB, D_GROUP), 1) // HEAD_DIM


def _block_diag(vblk):
    hol = _head_of_lane()
    return [jnp.where(hol == h, vblk, jnp.zeros_like(vblk)) for h in range(N_HEADS)]


def _local_chunk(zr, s0, za, zb, pbuf, hbuf, wcat, vg_ref, bs_ref, wsc_ref, wcc_ref, lng_ref, lnb_ref):
    g = D_GROUP
    r0 = s0 + zr
    ga = _gelu_tanh(za[zr:zr + CHUNK, :])
    u = ga[:, 0:g]
    v = (_rms(ga[:, g:2 * g]) * vg_ref[...]).astype(BF16)
    f = jnp.dot(wcat, jnp.concatenate(_block_diag(v), axis=0), preferred_element_type=F32) + bs_ref[...]
    ya = u * f
    conv_b = wsc_ref[SHORT_W - 1:SHORT_W, :] * pbuf[SHORT_HALO + r0:SHORT_HALO + r0 + CHUNK, :]
    for kk in range(SHORT_W - 1):
        off = SHORT_HALO + r0 - (SHORT_W - 1) + kk
        conv_b = conv_b + wsc_ref[kk:kk + 1, :] * pbuf[off:off + CHUNK, :]
    yb = zb[zr:zr + CHUNK, 0:g] * conv_b
    base = CONF_HALO + r0 - (CONF_W - 1)
    conv_d = None
    for res in range(SUBLANES):
        part = None
        for kk in range(CONF_W):
            if (base + kk) % SUBLANES != res:
                continue
            off = base + kk - res
            term = wcc_ref[kk:kk + 1, :] * hbuf[off:off + CHUNK + SUBLANES, :]
            part = term if part is None else part + term
        if part is None:
            continue
        shifted = part[res:res + CHUNK, :]
        conv_d = shifted if conv_d is None else conv_d + shifted
    xc = conv_d - jnp.mean(conv_d, axis=-1, keepdims=True)
    var = jnp.mean(xc * xc, axis=-1, keepdims=True)
    ln = xc * lax.rsqrt(var + EPS) * lng_ref[...] + lnb_ref[...]
    yd = ln * jax.nn.sigmoid(ln)
    return ya, yb, yd


def _proj_local_kernel(x_ref, gin_ref, win_ref, vg_ref, ws_ref, bs_ref, wsc_ref, wcc_ref, lng_ref, lnb_ref,
                       go_ref, yl_ref, q_ref, k_ref, v_ref, pbuf, hbuf):
    g = D_GROUP
    t = TMP

    @pl.when(pl.program_id(1) == 0)
    def _():
        pbuf[0:SHORT_HALO, :] = jnp.zeros((SHORT_HALO, g), F32)
        hbuf[...] = jnp.zeros(hbuf.shape, F32)

    rr = lax.broadcasted_iota(jnp.int32, (CHUNK, N_HEADS * CHUNK), 0)
    cc = lax.broadcasted_iota(jnp.int32, (CHUNK, N_HEADS * CHUNK), 1)
    wcat = jnp.where((cc % CHUNK) <= rr, ws_ref[...], 0.0).astype(BF16)

    nsub = TMP // SUB

    def proj(sub, lo, hi):
        return jnp.dot(hs[sub], win_ref[:, lo:hi], preferred_element_type=F32)

    hs, zas, zbs = [], [], []
    for sub in range(nsub):
        s0 = sub * SUB
        hs.append((_rms(x_ref[s0:s0 + SUB, :]) * gin_ref[...]).astype(BF16))
        zd = proj(sub, 8 * g, 10 * g)
        hbuf[CONF_HALO + s0:CONF_HALO + s0 + SUB, :] = zd[:, 0:g] * jax.nn.sigmoid(zd[:, g:2 * g])
    for sub in range(nsub):
        s0 = sub * SUB
        zb = proj(sub, 2 * g, 5 * g)
        zbs.append(zb)
        pbuf[SHORT_HALO + s0:SHORT_HALO + s0 + SUB, :] = zb[:, g:2 * g] * zb[:, 2 * g:3 * g]
    for sub in range(nsub):
        zas.append(proj(sub, 0, 2 * g))
    for sub in range(nsub):
        s0 = sub * SUB
        for cs in range(SUB // CHUNK):
            r0 = s0 + cs * CHUNK
            ya, yb, yd = _local_chunk(cs * CHUNK, s0, zas[sub], zbs[sub], pbuf, hbuf, wcat,
                                      vg_ref, bs_ref, wsc_ref, wcc_ref, lng_ref, lnb_ref)
            yl_ref[r0:r0 + CHUNK, 0:g] = (_rms(ya) * go_ref[:, 0:g]).astype(BF16)
            yl_ref[r0:r0 + CHUNK, g:2 * g] = (_rms(yb) * go_ref[:, g:2 * g]).astype(BF16)
            yl_ref[r0:r0 + CHUNK, 2 * g:3 * g] = (_rms(yd) * go_ref[:, 2 * g:3 * g]).astype(BF16)
    for sub in range(nsub):
        s0 = sub * SUB
        qkv = proj(sub, 5 * g, 8 * g)
        q_ref[s0:s0 + SUB, :] = (qkv[:, 0:g] * -(HEAD_DIM ** -0.5)).astype(BF16)
        k_ref[s0:s0 + SUB, :] = qkv[:, g:2 * g].astype(BF16)
        v_ref[s0:s0 + SUB, :] = qkv[:, 2 * g:3 * g].astype(BF16)

    pbuf[0:SHORT_HALO, :] = pbuf[t:t + SHORT_HALO, :]
    hbuf[0:CONF_HALO, :] = hbuf[t:t + CONF_HALO, :]


def _proj_local(l, x, g_in, w_in, vg, ws_cat, bs_lanes, w_short, w_conf, ln_g, ln_b, g_loc):
    b, s, d = x.shape
    g = D_GROUP
    tile = lambda bi, i: (bi, i, 0)
    const = lambda bi, i: (0, 0)
    return pl.pallas_call(
        _proj_local_kernel,
        grid=(b, s // TMP),
        in_specs=[
            pl.BlockSpec((None, TMP, d), tile),
            pl.BlockSpec((1, d), const),
            pl.BlockSpec((None, d, D_IN_PROJ), lambda bi, i: (l, 0, 0), pipeline_mode=pl.Buffered(1)),
            pl.BlockSpec((1, g), const),
            pl.BlockSpec((CHUNK, N_HEADS * CHUNK), const),
            pl.BlockSpec((CHUNK, g), const),
            pl.BlockSpec((SHORT_W, g), const),
            pl.BlockSpec((CONF_W, g), const),
            pl.BlockSpec((1, g), const),
            pl.BlockSpec((1, g), const),
            pl.BlockSpec((1, 3 * g), const),
        ],
        out_specs=[
            pl.BlockSpec((None, TMP, 3 * g), tile),
            pl.BlockSpec((None, TMP, g), tile),
            pl.BlockSpec((None, TMP, g), tile),
            pl.BlockSpec((None, TMP, g), tile),
        ],
        out_shape=[
            jax.ShapeDtypeStruct((b, s, 3 * g), BF16),
            jax.ShapeDtypeStruct((b, s, g), BF16),
            jax.ShapeDtypeStruct((b, s, g), BF16),
            jax.ShapeDtypeStruct((b, s, g), BF16),
        ],
        scratch_shapes=[
            pltpu.VMEM((SHORT_HALO + TMP, g), F32),
            pltpu.VMEM((CONF_HALO + TMP + SUBLANES, g), F32),
        ],
        compiler_params=pltpu.CompilerParams(
            dimension_semantics=("arbitrary", "arbitrary"), vmem_limit_bytes=40 * MIB),
        name="proj_local",
    )(x, g_in, w_in, vg, ws_cat, bs_lanes, w_short, w_conf, ln_g, ln_b, g_loc)


def _stack_heads(q):
    return jnp.concatenate(_block_diag(q), axis=0)


def _suffix_matrix():
    rr = lax.broadcasted_iota(jnp.int32, (2 * QB, 2 * QB), 0) % QB
    cc = lax.broadcasted_iota(jnp.int32, (2 * QB, 2 * QB), 1)
    return jnp.where((cc >= QB) | (rr > cc), 1.0, 0.0).astype(BF16)


def _neg_softplus(ns):
    return jnp.minimum(ns, 0.0) - jnp.log(1.0 + jnp.exp2(jnp.abs(ns) * (-math.log2(math.e))))


def _split(lom):
    hi = lom.astype(BF16)
    lo = (lom - hi.astype(F32)).astype(BF16)
    return jnp.concatenate([hi, lo], axis=1)


def _heads_to_lanes(wb):
    return [wb[h * QB:(h + 1) * QB, :] for h in range(N_HEADS)]


class _Window:
    def __init__(self, q, kblks, vblks, exists, tt):
        self.q, self.kblks, self.vblks, self.exists, self.tt = q, kblks, vblks, exists, tt
        self.nw = len(kblks)
        nh = N_HEADS
        qrow = lax.broadcasted_iota(jnp.int32, (nh * QB, QB), 0) % QB
        kcol = lax.broadcasted_iota(jnp.int32, (nh * QB, QB), 1)
        self.causal = kcol < qrow

    def _mask(self, j, val):
        if j == self.nw - 1:
            return jnp.where(self.causal, val, 0.0)
        return val if self.exists[j] is None else jnp.where(self.exists[j], val, 0.0)

    def scores(self):
        self.ns = lax.dot_general(_stack_heads(self.q), jnp.concatenate(self.kblks, axis=0),
                                  (((1,), (1,)), ((), ())), preferred_element_type=F32)
        return self.ns

    def suffix_sums(self):
        nw = self.nw
        self.ns_j = [self.ns[:, j * QB:(j + 1) * QB] for j in range(nw)]
        self.lom_j = [self._mask(j, _neg_softplus(self.ns_j[j])) for j in range(nw)]
        self.ll = jnp.dot(jnp.concatenate([_split(l) for l in self.lom_j], axis=0), self.tt,
                          preferred_element_type=F32)
        return self.ll

    def weights(self):
        nw, nh = self.nw, N_HEADS
        later = None
        wcols = [None] * nw
        for j in reversed(range(nw)):
            rows = slice(j * nh * QB, (j + 1) * nh * QB)
            logw = (self.lom_j[j] - self.ns_j[j]) + self.ll[rows, 0:QB]
            if later is not None:
                logw = logw + later
            wcols[j] = _heads_to_lanes(self._mask(j, jnp.exp(logw)).astype(BF16))
            tot = self.ll[rows, QB:2 * QB]
            later = tot if later is None else later + tot
        self.wcat = jnp.concatenate([c for cols in wcols for c in cols], axis=1)
        return later

    def values(self):
        vbd = jnp.concatenate([m for vb in self.vblks for m in _block_diag(vb)], axis=0)
        return jnp.dot(self.wcat, vbd, preferred_element_type=F32)


def _after(x, vals):
    z = sum(jnp.minimum(jnp.maximum(v[0:1, 0:CHUNK], 0.0), 0.0) for v in vals)
    return jnp.concatenate([x[:, 0:CHUNK] + z.astype(x.dtype), x[:, CHUNK:]], axis=1)


def _attn_block(qs, kblk, vblk, tt, carry):
    nsb = lax.dot_general(qs, kblk, (((1,), (1,)), ((), ())), preferred_element_type=F32)
    lom = _neg_softplus(nsb)
    llb = jnp.dot(_split(lom), tt, preferred_element_type=F32)
    w = jnp.exp((lom - nsb) + llb[:, 0:QB] + carry)
    wcat = jnp.concatenate(_heads_to_lanes(w.astype(BF16)), axis=1)
    vbd = jnp.concatenate(_block_diag(vblk), axis=0)
    return jnp.dot(wcat, vbd, preferred_element_type=F32), llb[:, QB:2 * QB]


def _attn_ffn_kernel(x_ref, yl_ref, q_ref, kc_ref, vc_ref, kp_ref, vp_ref, k_hbm, v_hbm,
                     wo_ref, gf_ref, wup_ref, wdn_ref, gfin_ref, gc_ref,
                     o_ref, yc_scr, carry_scr, acc_scr, kbuf, vbuf, sem, *, n_tiles, final_norm):
    bi = pl.program_id(0)
    i = pl.program_id(1)
    g = D_GROUP

    def step(do_attn, do_mlp):
        if do_attn:
            tt = _suffix_matrix()
            has_prev = i > 0
            wins = []
            for c in range(NQ):
                kblks, vblks, exists = [], [], []
                for j in range(ATTN_WINDOW):
                    rel = c - (ATTN_WINDOW - 1) + j
                    kr, vr, r = (kc_ref, vc_ref, rel) if rel >= 0 else (kp_ref, vp_ref, NQ + rel)
                    kblks.append(kr[r * QB:(r + 1) * QB, :])
                    vblks.append(vr[r * QB:(r + 1) * QB, :])
                    exists.append(None if rel >= 0 else has_prev)
                wins.append(_Window(q_ref[c * QB:(c + 1) * QB, :], kblks, vblks, exists, tt))

        order = _after if (do_attn and do_mlp) else (lambda val, _deps: val)
        if do_attn:
            ns_all = [w.scores() for w in wins]
        if do_mlp:
            yl = order(yl_ref[...], ns_all if do_attn else None)
            y = jnp.concatenate([yl[:, 0:2 * g], yc_scr[...], yl[:, 2 * g:3 * g]], axis=1)
            x1 = x_ref[...] + jnp.dot(y, wo_ref[...], preferred_element_type=F32)
        if do_attn:
            ll_all = [w.suffix_sums() for w in wins]
        if do_mlp:
            h = order(_rms(x1) * gf_ref[...], ll_all if do_attn else None).astype(BF16)
            a = jnp.maximum(jnp.dot(h, wup_ref[...], preferred_element_type=F32), 0.0)
        if do_attn:
            carries = [w.weights() for w in wins]
            for c in range(NQ):
                carry_scr[c] = carries[c]
            live_any = jnp.max(functools.reduce(jnp.maximum, carries))
            pv_all = [w.values() for w in wins]
            for c in range(NQ):
                acc_scr[c] = pv_all[c]
                yc_scr[c * QB:(c + 1) * QB, :] = (_rms(pv_all[c]) * gc_ref[...]).astype(BF16)
        if do_mlp:
            a2 = order(a * a, pv_all if do_attn else None).astype(BF16)
            x2 = x1 + jnp.dot(a2, wdn_ref[...], preferred_element_type=F32)
            if final_norm:
                x2 = _rms(x2) * gfin_ref[...]
            o_ref[...] = x2
        if not do_attn:
            return

        def fetch(src, dst, slot, kb):
            cp = pltpu.make_async_copy(src.at[bi, pl.ds(pl.multiple_of(kb * QB, QB), QB), :], dst, sem.at[slot])
            cp.start()
            return cp

        def older_blocks(c, _):
            qs = _stack_heads(q_ref[pl.ds(pl.multiple_of(c * QB, QB), QB), :])

            def cond(st):
                kb, live = st
                return jnp.logical_and(kb >= 0, live > -STICK_DEAD)

            def body(st):
                kb, _ = st
                ck = fetch(k_hbm, kbuf, 0, kb)
                cv = fetch(v_hbm, vbuf, 1, kb)
                ck.wait()
                cv.wait()
                carry = carry_scr[c]
                pv, tot = _attn_block(qs, kbuf[...], vbuf[...], tt, carry)
                acc_scr[c] = acc_scr[c] + pv
                carry_scr[c] = carry + tot
                return kb - 1, jnp.max(carry + tot)

            lax.while_loop(cond, body, (i * NQ + c - ATTN_WINDOW, jnp.max(carry_scr[c])))
            return 0

        @pl.when(live_any > -STICK_DEAD)
        def _():
            lax.fori_loop(0, NQ, older_blocks, 0)
            for c in range(NQ):
                yc_scr[c * QB:(c + 1) * QB, :] = (_rms(acc_scr[c]) * gc_ref[...]).astype(BF16)

    @pl.when(i == 0)
    def _():
        step(True, False)

    @pl.when(jnp.logical_and(i > 0, i < n_tiles))
    def _():
        step(True, True)

    @pl.when(i == n_tiles)
    def _():
        step(False, True)


def _attn_ffn(l, x, y_loc, q, k, v, w_out, g_ffn, w_up, w_dn, g_fin, g_att, final_norm):
    b, s, d = x.shape
    g = D_GROUP
    nt = s // TM
    lag = lambda bi, i: (bi, jnp.maximum(i - 1, 0), 0)
    cur = lambda bi, i: (bi, jnp.minimum(i, nt - 1), 0)
    prev = lambda bi, i: (bi, jnp.maximum(jnp.minimum(i, nt - 1) - 1, 0), 0)
    const = lambda bi, i: (0, 0)
    resident = lambda w: pl.BlockSpec((None,) + w.shape[1:], lambda bi, i: (l, 0, 0), pipeline_mode=pl.Buffered(1))
    return pl.pallas_call(
        functools.partial(_attn_ffn_kernel, n_tiles=nt, final_norm=final_norm),
        grid=(b, nt + 1),
        in_specs=[
            pl.BlockSpec((None, TM, d), lag),
            pl.BlockSpec((None, TM, 3 * g), lag),
            pl.BlockSpec((None, TM, g), cur),
            pl.BlockSpec((None, TM, g), cur),
            pl.BlockSpec((None, TM, g), cur),
            pl.BlockSpec((None, TM, g), prev),
            pl.BlockSpec((None, TM, g), prev),
            pl.BlockSpec(memory_space=pl.ANY),
            pl.BlockSpec(memory_space=pl.ANY),
            resident(w_out),
            pl.BlockSpec((1, d), const),
            resident(w_up),
            resident(w_dn),
            pl.BlockSpec((1, d), const),
            pl.BlockSpec((1, g), const),
        ],
        out_specs=pl.BlockSpec((None, TM, d), lag),
        out_shape=jax.ShapeDtypeStruct((b, s, d), F32),
        scratch_shapes=[
            pltpu.VMEM((TM, g), BF16),
            pltpu.VMEM((NQ, N_HEADS * QB, QB), F32),
            pltpu.VMEM((NQ, QB, g), F32),
            pltpu.VMEM((QB, g), BF16),
            pltpu.VMEM((QB, g), BF16),
            pltpu.SemaphoreType.DMA((2,)),
        ],
        compiler_params=pltpu.CompilerParams(
            dimension_semantics=("arbitrary", "arbitrary"), vmem_limit_bytes=56 * MIB),
        name="attn_ffn",
    )(x, y_loc, q, k, v, k, v, k, v, w_out, g_ffn, w_up, w_dn, g_fin, g_att)


def kernel(x, norm_mix_g, w_in, gmlp_v_g, gmlp_w_s, gmlp_b_s, short_conv_w, conf_conv_w, conf_ln_g, conf_ln_b,
           mix_out_g, w_out, norm_ffn_g, w_up, w_down, final_norm_g):
    depth = w_in.shape[0]
    g = D_GROUP
    w_in_b, w_out_b, w_up_b, w_down_b = (w.astype(BF16) for w in (w_in, w_out, w_up, w_down))
    for l in range(depth):
        ws_cat = jnp.transpose(gmlp_w_s[l], (1, 0, 2)).reshape(CHUNK, N_HEADS * CHUNK)
        bs_lanes = jnp.repeat(gmlp_b_s[l].T, HEAD_DIM, axis=1)
        go = mix_out_g[l]
        g_loc = jnp.concatenate([go[0:2 * g], go[3 * g:4 * g]])[None, :]
        y_loc, q, k, v = _proj_local(
            l, x, norm_mix_g[l][None, :], w_in_b, gmlp_v_g[l][None, :], ws_cat, bs_lanes,
            short_conv_w[l], conf_conv_w[l], conf_ln_g[l][None, :], conf_ln_b[l][None, :], g_loc)
        x = _attn_ffn(
            l, x, y_loc, q, k, v, w_out_b, norm_ffn_g[l][None, :], w_up_b, w_down_b,
            final_norm_g[None, :], go[2 * g:3 * g][None, :], final_norm=(l == depth - 1))
    return x
```

```python
import functools
import math

import jax
import jax.numpy as jnp
from jax import lax
from jax.experimental import pallas as pl
from jax.experimental.pallas import tpu as pltpu

D_MODEL = 1024
D_GROUP = 256
HEAD_DIM = 64
N_HEADS = D_GROUP // HEAD_DIM
CHUNK = 128
SHORT_W = 3
CONF_W = 31
D_FF = 4 * D_MODEL
D_IN_PROJ = 10 * D_GROUP
EPS = 1e-6

F32 = jnp.float32
BF16 = jnp.bfloat16

SUBLANES = 8
MIB = 1024 * 1024

TM = 512
TMP = 1024
SUB = 256
QB = CHUNK
NQ = TM // QB
ATTN_WINDOW = 3
SHORT_HALO = SUBLANES
CONF_HALO = 4 * SUBLANES
STICK_DEAD = 110.0


def _rms(x):
    return x * lax.rsqrt(jnp.mean(x * x, axis=-1, keepdims=True) + EPS)


def _gelu_tanh(x):
    c = math.sqrt(2.0 / math.pi)
    return 0.5 * x * (1.0 + jnp.tanh(c * (x + 0.044715 * (x * x * x))))


def _head_of_lane():
    return lax.broadcasted_iota(jnp.int32, (QB, D_GROUP), 1) // HEAD_DIM


def _block_diag(vblk):
    hol = _head_of_lane()
    return [jnp.where(hol == h, vblk, jnp.zeros_like(vblk)) for h in range(N_HEADS)]


def _local_chunk(zr, s0, za, zb, pbuf, hbuf, wcat, vg_ref, bs_ref, wsc_ref, wcc_ref, lng_ref, lnb_ref):
    g = D_GROUP
    r0 = s0 + zr
    ga = _gelu_tanh(za[zr:zr + CHUNK, :])
    u = ga[:, 0:g]
    v = (_rms(ga[:, g:2 * g]) * vg_ref[...]).astype(BF16)
    f = jnp.dot(wcat, jnp.concatenate(_block_diag(v), axis=0), preferred_element_type=F32) + bs_ref[...]
    ya = u * f
    conv_b = wsc_ref[SHORT_W - 1:SHORT_W, :] * pbuf[SHORT_HALO + r0:SHORT_HALO + r0 + CHUNK, :]
    for kk in range(SHORT_W - 1):
        off = SHORT_HALO + r0 - (SHORT_W - 1) + kk
        conv_b = conv_b + wsc_ref[kk:kk + 1, :] * pbuf[off:off + CHUNK, :]
    yb = zb[zr:zr + CHUNK, 0:g] * conv_b
    base = CONF_HALO + r0 - (CONF_W - 1)
    conv_d = None
    for res in range(SUBLANES):
        part = None
        for kk in range(CONF_W):
            if (base + kk) % SUBLANES != res:
                continue
            off = base + kk - res
            term = wcc_ref[kk:kk + 1, :] * hbuf[off:off + CHUNK + SUBLANES, :]
            part = term if part is None else part + term
        if part is None:
            continue
        shifted = part[res:res + CHUNK, :]
        conv_d = shifted if conv_d is None else conv_d + shifted
    xc = conv_d - jnp.mean(conv_d, axis=-1, keepdims=True)
    var = jnp.mean(xc * xc, axis=-1, keepdims=True)
    ln = xc * lax.rsqrt(var + EPS) * lng_ref[...] + lnb_ref[...]
    yd = ln * jax.nn.sigmoid(ln)
    return ya, yb, yd


def _proj_local_kernel(x_ref, gin_ref, win_ref, vg_ref, ws_ref, bs_ref, wsc_ref, wcc_ref, lng_ref, lnb_ref,
                       go_ref, yl_ref, q_ref, k_ref, v_ref, pbuf, hbuf):
    g = D_GROUP
    t = TMP

    @pl.when(pl.program_id(1) == 0)
    def _():
        pbuf[0:SHORT_HALO, :] = jnp.zeros((SHORT_HALO, g), F32)
        hbuf[...] = jnp.zeros(hbuf.shape, F32)

    rr = lax.broadcasted_iota(jnp.int32, (CHUNK, N_HEADS * CHUNK), 0)
    cc = lax.broadcasted_iota(jnp.int32, (CHUNK, N_HEADS * CHUNK), 1)
    wcat = jnp.where((cc % CHUNK) <= rr, ws_ref[...], 0.0).astype(BF16)

    nsub = TMP // SUB

    def proj(sub, lo, hi):
        return jnp.dot(hs[sub], win_ref[:, lo:hi], preferred_element_type=F32)

    hs, zas, zbs = [], [], []
    for sub in range(nsub):
        s0 = sub * SUB
        hs.append((_rms(x_ref[s0:s0 + SUB, :]) * gin_ref[...]).astype(BF16))
        zd = proj(sub, 8 * g, 10 * g)
        hbuf[CONF_HALO + s0:CONF_HALO + s0 + SUB, :] = zd[:, 0:g] * jax.nn.sigmoid(zd[:, g:2 * g])
    for sub in range(nsub):
        s0 = sub * SUB
        zb = proj(sub, 2 * g, 5 * g)
        zbs.append(zb)
        pbuf[SHORT_HALO + s0:SHORT_HALO + s0 + SUB, :] = zb[:, g:2 * g] * zb[:, 2 * g:3 * g]
    for sub in range(nsub):
        zas.append(proj(sub, 0, 2 * g))
    for sub in range(nsub):
        s0 = sub * SUB
        for cs in range(SUB // CHUNK):
            r0 = s0 + cs * CHUNK
            ya, yb, yd = _local_chunk(cs * CHUNK, s0, zas[sub], zbs[sub], pbuf, hbuf, wcat,
                                      vg_ref, bs_ref, wsc_ref, wcc_ref, lng_ref, lnb_ref)
            yl_ref[r0:r0 + CHUNK, 0:g] = (_rms(ya) * go_ref[:, 0:g]).astype(BF16)
            yl_ref[r0:r0 + CHUNK, g:2 * g] = (_rms(yb) * go_ref[:, g:2 * g]).astype(BF16)
            yl_ref[r0:r0 + CHUNK, 2 * g:3 * g] = (_rms(yd) * go_ref[:, 2 * g:3 * g]).astype(BF16)
    for sub in range(nsub):
        s0 = sub * SUB
        qkv = proj(sub, 5 * g, 8 * g)
        q_ref[s0:s0 + SUB, :] = (qkv[:, 0:g] * -(HEAD_DIM ** -0.5)).astype(BF16)
        k_ref[s0:s0 + SUB, :] = qkv[:, g:2 * g].astype(BF16)
        v_ref[s0:s0 + SUB, :] = qkv[:, 2 * g:3 * g].astype(BF16)

    pbuf[0:SHORT_HALO, :] = pbuf[t:t + SHORT_HALO, :]
    hbuf[0:CONF_HALO, :] = hbuf[t:t + CONF_HALO, :]


def _proj_local(x, g_in, w_in, vg, ws_cat, bs_lanes, w_short, w_conf, ln_g, ln_b, g_loc):
    b, s, d = x.shape
    g = D_GROUP
    tile = lambda bi, i: (bi, i, 0)
    const = lambda bi, i: (0, 0)
    return pl.pallas_call(
        _proj_local_kernel,
        grid=(b, s // TMP),
        in_specs=[
            pl.BlockSpec((None, TMP, d), tile),
            pl.BlockSpec((1, d), const),
            pl.BlockSpec((d, D_IN_PROJ), const, pipeline_mode=pl.Buffered(1)),
            pl.BlockSpec((1, g), const),
            pl.BlockSpec((CHUNK, N_HEADS * CHUNK), const),
            pl.BlockSpec((CHUNK, g), const),
            pl.BlockSpec((SHORT_W, g), const),
            pl.BlockSpec((CONF_W, g), const),
            pl.BlockSpec((1, g), const),
            pl.BlockSpec((1, g), const),
            pl.BlockSpec((1, 3 * g), const),
        ],
        out_specs=[
            pl.BlockSpec((None, TMP, 3 * g), tile),
            pl.BlockSpec((None, TMP, g), tile),
            pl.BlockSpec((None, TMP, g), tile),
            pl.BlockSpec((None, TMP, g), tile),
        ],
        out_shape=[
            jax.ShapeDtypeStruct((b, s, 3 * g), BF16),
            jax.ShapeDtypeStruct((b, s, g), BF16),
            jax.ShapeDtypeStruct((b, s, g), BF16),
            jax.ShapeDtypeStruct((b, s, g), BF16),
        ],
        scratch_shapes=[
            pltpu.VMEM((SHORT_HALO + TMP, g), F32),
            pltpu.VMEM((CONF_HALO + TMP + SUBLANES, g), F32),
        ],
        compiler_params=pltpu.CompilerParams(
            dimension_semantics=("arbitrary", "arbitrary"), vmem_limit_bytes=40 * MIB),
        name="proj_local",
    )(x, g_in, w_in, vg, ws_cat, bs_lanes, w_short, w_conf, ln_g, ln_b, g_loc)


def _stack_heads(q):
    return jnp.concatenate(_block_diag(q), axis=0)


def _suffix_matrix():
    rr = lax.broadcasted_iota(jnp.int32, (2 * QB, 2 * QB), 0) % QB
    cc = lax.broadcasted_iota(jnp.int32, (2 * QB, 2 * QB), 1)
    return jnp.where((cc >= QB) | (rr > cc), 1.0, 0.0).astype(BF16)


def _neg_softplus(ns):
    return jnp.minimum(ns, 0.0) - jnp.log(1.0 + jnp.exp2(jnp.abs(ns) * (-math.log2(math.e))))


def _split(lom):
    hi = lom.astype(BF16)
    lo = (lom - hi.astype(F32)).astype(BF16)
    return jnp.concatenate([hi, lo], axis=1)


def _heads_to_lanes(wb):
    return [wb[h * QB:(h + 1) * QB, :] for h in range(N_HEADS)]


class _Window:
    def __init__(self, q, kblks, vblks, exists, tt):
        self.q, self.kblks, self.vblks, self.exists, self.tt = q, kblks, vblks, exists, tt
        self.nw = len(kblks)
        nh = N_HEADS
        qrow = lax.broadcasted_iota(jnp.int32, (nh * QB, QB), 0) % QB
        kcol = lax.broadcasted_iota(jnp.int32, (nh * QB, QB), 1)
        self.causal = kcol < qrow

    def _mask(self, j, val):
        if j == self.nw - 1:
            return jnp.where(self.causal, val, 0.0)
        return val if self.exists[j] is None else jnp.where(self.exists[j], val, 0.0)

    def scores(self):
        self.ns = lax.dot_general(_stack_heads(self.q), jnp.concatenate(self.kblks, axis=0),
                                  (((1,), (1,)), ((), ())), preferred_element_type=F32)
        return self.ns

    def suffix_sums(self):
        nw = self.nw
        self.ns_j = [self.ns[:, j * QB:(j + 1) * QB] for j in range(nw)]
        self.lom_j = [self._mask(j, _neg_softplus(self.ns_j[j])) for j in range(nw)]
        self.ll = jnp.dot(jnp.concatenate([_split(l) for l in self.lom_j], axis=0), self.tt,
                          preferred_element_type=F32)
        return self.ll

    def weights(self):
        nw, nh = self.nw, N_HEADS
        later = None
        wcols = [None] * nw
        for j in reversed(range(nw)):
            rows = slice(j * nh * QB, (j + 1) * nh * QB)
            logw = (self.lom_j[j] - self.ns_j[j]) + self.ll[rows, 0:QB]
            if later is not None:
                logw = logw + later
            wcols[j] = _heads_to_lanes(self._mask(j, jnp.exp(logw)).astype(BF16))
            tot = self.ll[rows, QB:2 * QB]
            later = tot if later is None else later + tot
        self.wcat = jnp.concatenate([c for cols in wcols for c in cols], axis=1)
        return later

    def values(self):
        vbd = jnp.concatenate([m for vb in self.vblks for m in _block_diag(vb)], axis=0)
        return jnp.dot(self.wcat, vbd, preferred_element_type=F32)


def _after(x, vals):
    z = sum(jnp.minimum(jnp.maximum(v[0:1, 0:CHUNK], 0.0), 0.0) for v in vals)
    return jnp.concatenate([x[:, 0:CHUNK] + z.astype(x.dtype), x[:, CHUNK:]], axis=1)


def _attn_block(qs, kblk, vblk, tt, carry):
    nsb = lax.dot_general(qs, kblk, (((1,), (1,)), ((), ())), preferred_element_type=F32)
    lom = _neg_softplus(nsb)
    llb = jnp.dot(_split(lom), tt, preferred_element_type=F32)
    w = jnp.exp((lom - nsb) + llb[:, 0:QB] + carry)
    wcat = jnp.concatenate(_heads_to_lanes(w.astype(BF16)), axis=1)
    vbd = jnp.concatenate(_block_diag(vblk), axis=0)
    return jnp.dot(wcat, vbd, preferred_element_type=F32), llb[:, QB:2 * QB]


def _attn_ffn_kernel(x_ref, yl_ref, q_ref, kc_ref, vc_ref, kp_ref, vp_ref, k_hbm, v_hbm,
                     wo_ref, gf_ref, wup_ref, wdn_ref, gfin_ref, gc_ref,
                     o_ref, yc_scr, carry_scr, acc_scr, kbuf, vbuf, sem, *, n_tiles, final_norm):
    bi = pl.program_id(0)
    i = pl.program_id(1)
    g = D_GROUP

    def step(do_attn, do_mlp):
        if do_attn:
            tt = _suffix_matrix()
            has_prev = i > 0
            wins = []
            for c in range(NQ):
                kblks, vblks, exists = [], [], []
                for j in range(ATTN_WINDOW):
                    rel = c - (ATTN_WINDOW - 1) + j
                    kr, vr, r = (kc_ref, vc_ref, rel) if rel >= 0 else (kp_ref, vp_ref, NQ + rel)
                    kblks.append(kr[r * QB:(r + 1) * QB, :])
                    vblks.append(vr[r * QB:(r + 1) * QB, :])
                    exists.append(None if rel >= 0 else has_prev)
                wins.append(_Window(q_ref[c * QB:(c + 1) * QB, :], kblks, vblks, exists, tt))

        order = _after if (do_attn and do_mlp) else (lambda val, _deps: val)
        if do_attn:
            ns_all = [w.scores() for w in wins]
        if do_mlp:
            yl = order(yl_ref[...], ns_all if do_attn else None)
            y = jnp.concatenate([yl[:, 0:2 * g], yc_scr[...], yl[:, 2 * g:3 * g]], axis=1)
            x1 = x_ref[...] + jnp.dot(y, wo_ref[...], preferred_element_type=F32)
        if do_attn:
            ll_all = [w.suffix_sums() for w in wins]
        if do_mlp:
            h = order(_rms(x1) * gf_ref[...], ll_all if do_attn else None).astype(BF16)
            a = jnp.maximum(jnp.dot(h, wup_ref[...], preferred_element_type=F32), 0.0)
        if do_attn:
            carries = [w.weights() for w in wins]
            for c in range(NQ):
                carry_scr[c] = carries[c]
            live_any = jnp.max(functools.reduce(jnp.maximum, carries))
            pv_all = [w.values() for w in wins]
            for c in range(NQ):
                acc_scr[c] = pv_all[c]
                yc_scr[c * QB:(c + 1) * QB, :] = (_rms(pv_all[c]) * gc_ref[...]).astype(BF16)
        if do_mlp:
            a2 = order(a * a, pv_all if do_attn else None).astype(BF16)
            x2 = x1 + jnp.dot(a2, wdn_ref[...], preferred_element_type=F32)
            if final_norm:
                x2 = _rms(x2) * gfin_ref[...]
            o_ref[...] = x2
        if not do_attn:
            return

        def fetch(src, dst, slot, kb):
            cp = pltpu.make_async_copy(src.at[bi, pl.ds(pl.multiple_of(kb * QB, QB), QB), :], dst, sem.at[slot])
            cp.start()
            return cp

        def older_blocks(c, _):
            qs = _stack_heads(q_ref[pl.ds(pl.multiple_of(c * QB, QB), QB), :])

            def cond(st):
                kb, live = st
                return jnp.logical_and(kb >= 0, live > -STICK_DEAD)

            def body(st):
                kb, _ = st
                ck = fetch(k_hbm, kbuf, 0, kb)
                cv = fetch(v_hbm, vbuf, 1, kb)
                ck.wait()
                cv.wait()
                carry = carry_scr[c]
                pv, tot = _attn_block(qs, kbuf[...], vbuf[...], tt, carry)
                acc_scr[c] = acc_scr[c] + pv
                carry_scr[c] = carry + tot
                return kb - 1, jnp.max(carry + tot)

            lax.while_loop(cond, body, (i * NQ + c - ATTN_WINDOW, jnp.max(carry_scr[c])))
            return 0

        @pl.when(live_any > -STICK_DEAD)
        def _():
            lax.fori_loop(0, NQ, older_blocks, 0)
            for c in range(NQ):
                yc_scr[c * QB:(c + 1) * QB, :] = (_rms(acc_scr[c]) * gc_ref[...]).astype(BF16)

    @pl.when(i == 0)
    def _():
        step(True, False)

    @pl.when(jnp.logical_and(i > 0, i < n_tiles))
    def _():
        step(True, True)

    @pl.when(i == n_tiles)
    def _():
        step(False, True)


def _attn_ffn(l, x, y_loc, q, k, v, w_out, g_ffn, w_up, w_dn, g_fin, g_att, final_norm):
    b, s, d = x.shape
    g = D_GROUP
    nt = s // TM
    lag = lambda bi, i: (bi, jnp.maximum(i - 1, 0), 0)
    cur = lambda bi, i: (bi, jnp.minimum(i, nt - 1), 0)
    prev = lambda bi, i: (bi, jnp.maximum(jnp.minimum(i, nt - 1) - 1, 0), 0)
    const = lambda bi, i: (0, 0)
    resident = lambda w: pl.BlockSpec((None,) + w.shape[1:], lambda bi, i: (l, 0, 0), pipeline_mode=pl.Buffered(1))
    return pl.pallas_call(
        functools.partial(_attn_ffn_kernel, n_tiles=nt, final_norm=final_norm),
        grid=(b, nt + 1),
        in_specs=[
            pl.BlockSpec((None, TM, d), lag),
            pl.BlockSpec((None, TM, 3 * g), lag),
            pl.BlockSpec((None, TM, g), cur),
            pl.BlockSpec((None, TM, g), cur),
            pl.BlockSpec((None, TM, g), cur),
            pl.BlockSpec((None, TM, g), prev),
            pl.BlockSpec((None, TM, g), prev),
            pl.BlockSpec(memory_space=pl.ANY),
            pl.BlockSpec(memory_space=pl.ANY),
            resident(w_out),
            pl.BlockSpec((1, d), const),
            resident(w_up),
            resident(w_dn),
            pl.BlockSpec((1, d), const),
            pl.BlockSpec((1, g), const),
        ],
        out_specs=pl.BlockSpec((None, TM, d), lag),
        out_shape=jax.ShapeDtypeStruct((b, s, d), F32),
        scratch_shapes=[
            pltpu.VMEM((TM, g), BF16),
            pltpu.VMEM((NQ, N_HEADS * QB, QB), F32),
            pltpu.VMEM((NQ, QB, g), F32),
            pltpu.VMEM((QB, g), BF16),
            pltpu.VMEM((QB, g), BF16),
            pltpu.SemaphoreType.DMA((2,)),
        ],
        compiler_params=pltpu.CompilerParams(
            dimension_semantics=("arbitrary", "arbitrary"), vmem_limit_bytes=56 * MIB),
        name="attn_ffn",
    )(x, y_loc, q, k, v, k, v, k, v, w_out, g_ffn, w_up, w_dn, g_fin, g_att)


def kernel(x, norm_mix_g, w_in, gmlp_v_g, gmlp_w_s, gmlp_b_s, short_conv_w, conf_conv_w, conf_ln_g, conf_ln_b,
           mix_out_g, w_out, norm_ffn_g, w_up, w_down, final_norm_g):
    depth = w_in.shape[0]
    g = D_GROUP
    w_out_b, w_up_b, w_down_b = (w.astype(BF16) for w in (w_out, w_up, w_down))
    for l in range(depth):
        ws_cat = jnp.transpose(gmlp_w_s[l], (1, 0, 2)).reshape(CHUNK, N_HEADS * CHUNK)
        bs_lanes = jnp.repeat(gmlp_b_s[l].T, HEAD_DIM, axis=1)
        go = mix_out_g[l]
        g_loc = jnp.concatenate([go[0:2 * g], go[3 * g:4 * g]])[None, :]
        y_loc, q, k, v = _proj_local(
            x, norm_mix_g[l][None, :], w_in[l].astype(BF16), gmlp_v_g[l][None, :], ws_cat, bs_lanes,
            short_conv_w[l], conf_conv_w[l], conf_ln_g[l][None, :], conf_ln_b[l][None, :], g_loc)
        x = _attn_ffn(
            l, x, y_loc, q, k, v, w_out_b, norm_ffn_g[l][None, :], w_up_b, w_down_b,
            final_norm_g[None, :], go[2 * g:3 * g][None, :], final_norm=(l == depth - 1))
    return x
```

```python
import functools
import math

import jax
import jax.numpy as jnp
from jax import lax
from jax.experimental import pallas as pl
from jax.experimental.pallas import tpu as pltpu

D_MODEL = 1024
D_GROUP = 256
HEAD_DIM = 64
N_HEADS = D_GROUP // HEAD_DIM
CHUNK = 128
SHORT_W = 3
CONF_W = 31
D_FF = 4 * D_MODEL
D_IN_PROJ = 10 * D_GROUP
EPS = 1e-6

F32 = jnp.float32
BF16 = jnp.bfloat16

SUBLANES = 8
MIB = 1024 * 1024

TM = 512
TMP = 1024
SUB = 256
QB = CHUNK
NQ = TM // QB
ATTN_WINDOW = 3
WEIGHT_STAGE_ROWS = 512
SHORT_HALO = SUBLANES
CONF_HALO = 4 * SUBLANES
STICK_DEAD = 110.0


def _rms(x):
    return x * lax.rsqrt(jnp.mean(x * x, axis=-1, keepdims=True) + EPS)


def _gelu_tanh(x):
    c = math.sqrt(2.0 / math.pi)
    return 0.5 * x * (1.0 + jnp.tanh(c * (x + 0.044715 * (x * x * x))))


def _head_of_lane():
    return lax.broadcasted_iota(jnp.int32, (QB, D_GROUP), 1) // HEAD_DIM


def _block_diag(vblk):
    hol = _head_of_lane()
    return [jnp.where(hol == h, vblk, jnp.zeros_like(vblk)) for h in range(N_HEADS)]


def _local_chunk(zr, s0, za, zb, pbuf, hbuf, wcat, vg_ref, bs_ref, wsc_ref, wcc_ref, lng_ref, lnb_ref):
    g = D_GROUP
    r0 = s0 + zr
    ga = _gelu_tanh(za[zr:zr + CHUNK, :])
    u = ga[:, 0:g]
    v = (_rms(ga[:, g:2 * g]) * vg_ref[...]).astype(BF16)
    f = jnp.dot(wcat, jnp.concatenate(_block_diag(v), axis=0), preferred_element_type=F32) + bs_ref[...]
    ya = u * f
    conv_b = wsc_ref[SHORT_W - 1:SHORT_W, :] * pbuf[SHORT_HALO + r0:SHORT_HALO + r0 + CHUNK, :]
    for kk in range(SHORT_W - 1):
        off = SHORT_HALO + r0 - (SHORT_W - 1) + kk
        conv_b = conv_b + wsc_ref[kk:kk + 1, :] * pbuf[off:off + CHUNK, :]
    yb = zb[zr:zr + CHUNK, 0:g] * conv_b
    base = CONF_HALO + r0 - (CONF_W - 1)
    conv_d = None
    for res in range(SUBLANES):
        part = None
        for kk in range(CONF_W):
            if (base + kk) % SUBLANES != res:
                continue
            off = base + kk - res
            term = wcc_ref[kk:kk + 1, :] * hbuf[off:off + CHUNK + SUBLANES, :]
            part = term if part is None else part + term
        if part is None:
            continue
        shifted = part[res:res + CHUNK, :]
        conv_d = shifted if conv_d is None else conv_d + shifted
    xc = conv_d - jnp.mean(conv_d, axis=-1, keepdims=True)
    var = jnp.mean(xc * xc, axis=-1, keepdims=True)
    ln = xc * lax.rsqrt(var + EPS) * lng_ref[...] + lnb_ref[...]
    yd = ln * jax.nn.sigmoid(ln)
    return ya, yb, yd


def _proj_local_kernel(x_ref, gin_ref, win_ref, vg_ref, ws_ref, bs_ref, wsc_ref, wcc_ref, lng_ref, lnb_ref,
                       go_ref, yl_ref, q_ref, k_ref, v_ref, pbuf, hbuf):
    g = D_GROUP
    t = TMP

    @pl.when(pl.program_id(1) == 0)
    def _():
        pbuf[0:SHORT_HALO, :] = jnp.zeros((SHORT_HALO, g), F32)
        hbuf[...] = jnp.zeros(hbuf.shape, F32)

    rr = lax.broadcasted_iota(jnp.int32, (CHUNK, N_HEADS * CHUNK), 0)
    cc = lax.broadcasted_iota(jnp.int32, (CHUNK, N_HEADS * CHUNK), 1)
    wcat = jnp.where((cc % CHUNK) <= rr, ws_ref[...], 0.0).astype(BF16)

    nsub = TMP // SUB

    def proj(sub, lo, hi):
        return jnp.dot(hs[sub], win_ref[:, lo:hi], preferred_element_type=F32)

    hs, zas, zbs = [], [], []
    for sub in range(nsub):
        s0 = sub * SUB
        hs.append((_rms(x_ref[s0:s0 + SUB, :]) * gin_ref[...]).astype(BF16))
        zd = proj(sub, 8 * g, 10 * g)
        hbuf[CONF_HALO + s0:CONF_HALO + s0 + SUB, :] = zd[:, 0:g] * jax.nn.sigmoid(zd[:, g:2 * g])
    for sub in range(nsub):
        s0 = sub * SUB
        zb = proj(sub, 2 * g, 5 * g)
        zbs.append(zb)
        pbuf[SHORT_HALO + s0:SHORT_HALO + s0 + SUB, :] = zb[:, g:2 * g] * zb[:, 2 * g:3 * g]
    for sub in range(nsub):
        zas.append(proj(sub, 0, 2 * g))
    for sub in range(nsub):
        s0 = sub * SUB
        for cs in range(SUB // CHUNK):
            r0 = s0 + cs * CHUNK
            ya, yb, yd = _local_chunk(cs * CHUNK, s0, zas[sub], zbs[sub], pbuf, hbuf, wcat,
                                      vg_ref, bs_ref, wsc_ref, wcc_ref, lng_ref, lnb_ref)
            yl_ref[r0:r0 + CHUNK, 0:g] = (_rms(ya) * go_ref[:, 0:g]).astype(BF16)
            yl_ref[r0:r0 + CHUNK, g:2 * g] = (_rms(yb) * go_ref[:, g:2 * g]).astype(BF16)
            yl_ref[r0:r0 + CHUNK, 2 * g:3 * g] = (_rms(yd) * go_ref[:, 2 * g:3 * g]).astype(BF16)
    for sub in range(nsub):
        s0 = sub * SUB
        qkv = proj(sub, 5 * g, 8 * g)
        q_ref[s0:s0 + SUB, :] = (qkv[:, 0:g] * -(HEAD_DIM ** -0.5)).astype(BF16)
        k_ref[s0:s0 + SUB, :] = qkv[:, g:2 * g].astype(BF16)
        v_ref[s0:s0 + SUB, :] = qkv[:, 2 * g:3 * g].astype(BF16)

    pbuf[0:SHORT_HALO, :] = pbuf[t:t + SHORT_HALO, :]
    hbuf[0:CONF_HALO, :] = hbuf[t:t + CONF_HALO, :]


def _proj_local(x, g_in, w_in, vg, ws_cat, bs_lanes, w_short, w_conf, ln_g, ln_b, g_loc):
    b, s, d = x.shape
    g = D_GROUP
    tile = lambda bi, i: (bi, i, 0)
    const = lambda bi, i: (0, 0)
    return pl.pallas_call(
        _proj_local_kernel,
        grid=(b, s // TMP),
        in_specs=[
            pl.BlockSpec((None, TMP, d), tile),
            pl.BlockSpec((1, d), const),
            pl.BlockSpec((d, D_IN_PROJ), const, pipeline_mode=pl.Buffered(1)),
            pl.BlockSpec((1, g), const),
            pl.BlockSpec((CHUNK, N_HEADS * CHUNK), const),
            pl.BlockSpec((CHUNK, g), const),
            pl.BlockSpec((SHORT_W, g), const),
            pl.BlockSpec((CONF_W, g), const),
            pl.BlockSpec((1, g), const),
            pl.BlockSpec((1, g), const),
            pl.BlockSpec((1, 3 * g), const),
        ],
        out_specs=[
            pl.BlockSpec((None, TMP, 3 * g), tile),
            pl.BlockSpec((None, TMP, g), tile),
            pl.BlockSpec((None, TMP, g), tile),
            pl.BlockSpec((None, TMP, g), tile),
        ],
        out_shape=[
            jax.ShapeDtypeStruct((b, s, 3 * g), BF16),
            jax.ShapeDtypeStruct((b, s, g), BF16),
            jax.ShapeDtypeStruct((b, s, g), BF16),
            jax.ShapeDtypeStruct((b, s, g), BF16),
        ],
        scratch_shapes=[
            pltpu.VMEM((SHORT_HALO + TMP, g), F32),
            pltpu.VMEM((CONF_HALO + TMP + SUBLANES, g), F32),
        ],
        compiler_params=pltpu.CompilerParams(
            dimension_semantics=("arbitrary", "arbitrary"), vmem_limit_bytes=40 * MIB),
        name="proj_local",
    )(x, g_in, w_in, vg, ws_cat, bs_lanes, w_short, w_conf, ln_g, ln_b, g_loc)


def _stack_heads(q):
    return jnp.concatenate(_block_diag(q), axis=0)


def _suffix_matrix():
    rr = lax.broadcasted_iota(jnp.int32, (2 * QB, 2 * QB), 0) % QB
    cc = lax.broadcasted_iota(jnp.int32, (2 * QB, 2 * QB), 1)
    return jnp.where((cc >= QB) | (rr > cc), 1.0, 0.0).astype(BF16)


def _neg_softplus(ns):
    return jnp.minimum(ns, 0.0) - jnp.log(1.0 + jnp.exp2(jnp.abs(ns) * (-math.log2(math.e))))


def _split(lom):
    hi = lom.astype(BF16)
    lo = (lom - hi.astype(F32)).astype(BF16)
    return jnp.concatenate([hi, lo], axis=1)


def _heads_to_lanes(wb):
    return [wb[h * QB:(h + 1) * QB, :] for h in range(N_HEADS)]


class _Window:
    def __init__(self, q, kblks, vblks, exists, tt):
        self.q, self.kblks, self.vblks, self.exists, self.tt = q, kblks, vblks, exists, tt
        self.nw = len(kblks)
        nh = N_HEADS
        qrow = lax.broadcasted_iota(jnp.int32, (nh * QB, QB), 0) % QB
        kcol = lax.broadcasted_iota(jnp.int32, (nh * QB, QB), 1)
        self.causal = kcol < qrow

    def _mask(self, j, val):
        if j == self.nw - 1:
            return jnp.where(self.causal, val, 0.0)
        return val if self.exists[j] is None else jnp.where(self.exists[j], val, 0.0)

    def scores(self):
        self.ns = lax.dot_general(_stack_heads(self.q), jnp.concatenate(self.kblks, axis=0),
                                  (((1,), (1,)), ((), ())), preferred_element_type=F32)
        return self.ns

    def suffix_sums(self):
        nw = self.nw
        self.ns_j = [self.ns[:, j * QB:(j + 1) * QB] for j in range(nw)]
        self.lom_j = [self._mask(j, _neg_softplus(self.ns_j[j])) for j in range(nw)]
        self.ll = jnp.dot(jnp.concatenate([_split(l) for l in self.lom_j], axis=0), self.tt,
                          preferred_element_type=F32)
        return self.ll

    def weights(self):
        nw, nh = self.nw, N_HEADS
        later = None
        wcols = [None] * nw
        for j in reversed(range(nw)):
            rows = slice(j * nh * QB, (j + 1) * nh * QB)
            logw = (self.lom_j[j] - self.ns_j[j]) + self.ll[rows, 0:QB]
            if later is not None:
                logw = logw + later
            wcols[j] = _heads_to_lanes(self._mask(j, jnp.exp(logw)).astype(BF16))
            tot = self.ll[rows, QB:2 * QB]
            later = tot if later is None else later + tot
        self.wcat = jnp.concatenate([c for cols in wcols for c in cols], axis=1)
        return later

    def values(self):
        vbd = jnp.concatenate([m for vb in self.vblks for m in _block_diag(vb)], axis=0)
        return jnp.dot(self.wcat, vbd, preferred_element_type=F32)


def _after(x, vals):
    z = sum(jnp.minimum(jnp.maximum(v[0:1, 0:CHUNK], 0.0), 0.0) for v in vals)
    return jnp.concatenate([x[:, 0:CHUNK] + z.astype(x.dtype), x[:, CHUNK:]], axis=1)


def _attn_block(qs, kblk, vblk, tt, carry):
    nsb = lax.dot_general(qs, kblk, (((1,), (1,)), ((), ())), preferred_element_type=F32)
    lom = _neg_softplus(nsb)
    llb = jnp.dot(_split(lom), tt, preferred_element_type=F32)
    w = jnp.exp((lom - nsb) + llb[:, 0:QB] + carry)
    wcat = jnp.concatenate(_heads_to_lanes(w.astype(BF16)), axis=1)
    vbd = jnp.concatenate(_block_diag(vblk), axis=0)
    return jnp.dot(wcat, vbd, preferred_element_type=F32), llb[:, QB:2 * QB]


def _attn_ffn_kernel(x_ref, yl_ref, q_ref, kc_ref, vc_ref, kp_ref, vp_ref, k_hbm, v_hbm,
                     wo_hbm, gf_ref, wup_hbm, wdn_hbm, gfin_ref, gc_ref,
                     o_ref, yc_scr, carry_scr, acc_scr, kbuf, vbuf, sem,
                     wo_ref, wup_ref, wdn_ref, stage_sq, stage_up, wsem, *, layer, n_tiles, final_norm):
    bi = pl.program_id(0)
    i = pl.program_id(1)
    g = D_GROUP

    @pl.when(jnp.logical_and(bi == 0, i == 0))
    def _():
        rows_sq, rows_up = stage_sq.shape[1], stage_up.shape[1]
        jobs = []
        for src, dst, stage, s0, rows in ((wo_hbm, wo_ref, stage_sq, 0, rows_sq), (wdn_hbm, wdn_ref, stage_sq, 0, rows_sq),
                                          (wup_hbm, wup_ref, stage_up, 2, rows_up)):
            for r in range(dst.shape[0] // rows):
                jobs.append((src.at[layer, pl.ds(r * rows, rows), :], stage, s0, dst, r * rows, rows))
        copies = []
        for n, (src, stage, s0, _, _, _) in enumerate(jobs):
            copies.append(pltpu.make_async_copy(src, stage.at[n % 2], wsem.at[s0 + n % 2]))
        copies[0].start()
        for n, (_, stage, _, dst, r0, rows) in enumerate(jobs):
            if n + 1 < len(jobs):
                copies[n + 1].start()
            copies[n].wait()
            dst[r0:r0 + rows, :] = stage[n % 2].astype(BF16)

    def step(do_attn, do_mlp):
        if do_attn:
            tt = _suffix_matrix()
            has_prev = i > 0
            wins = []
            for c in range(NQ):
                kblks, vblks, exists = [], [], []
                for j in range(ATTN_WINDOW):
                    rel = c - (ATTN_WINDOW - 1) + j
                    kr, vr, r = (kc_ref, vc_ref, rel) if rel >= 0 else (kp_ref, vp_ref, NQ + rel)
                    kblks.append(kr[r * QB:(r + 1) * QB, :])
                    vblks.append(vr[r * QB:(r + 1) * QB, :])
                    exists.append(None if rel >= 0 else has_prev)
                wins.append(_Window(q_ref[c * QB:(c + 1) * QB, :], kblks, vblks, exists, tt))

        order = _after if (do_attn and do_mlp) else (lambda val, _deps: val)
        if do_attn:
            ns_all = [w.scores() for w in wins]
        if do_mlp:
            yl = order(yl_ref[...], ns_all if do_attn else None)
            y = jnp.concatenate([yl[:, 0:2 * g], yc_scr[...], yl[:, 2 * g:3 * g]], axis=1)
            x1 = x_ref[...] + jnp.dot(y, wo_ref[...], preferred_element_type=F32)
        if do_attn:
            ll_all = [w.suffix_sums() for w in wins]
        if do_mlp:
            h = order(_rms(x1) * gf_ref[...], ll_all if do_attn else None).astype(BF16)
            a = jnp.maximum(jnp.dot(h, wup_ref[...], preferred_element_type=F32), 0.0)
        if do_attn:
            carries = [w.weights() for w in wins]
            for c in range(NQ):
                carry_scr[c] = carries[c]
            live_any = jnp.max(functools.reduce(jnp.maximum, carries))
            pv_all = [w.values() for w in wins]
            for c in range(NQ):
                acc_scr[c] = pv_all[c]
                yc_scr[c * QB:(c + 1) * QB, :] = (_rms(pv_all[c]) * gc_ref[...]).astype(BF16)
        if do_mlp:
            a2 = order(a * a, pv_all if do_attn else None).astype(BF16)
            x2 = x1 + jnp.dot(a2, wdn_ref[...], preferred_element_type=F32)
            if final_norm:
                x2 = _rms(x2) * gfin_ref[...]
            o_ref[...] = x2
        if not do_attn:
            return

        def fetch(src, dst, slot, kb):
            cp = pltpu.make_async_copy(src.at[bi, pl.ds(pl.multiple_of(kb * QB, QB), QB), :], dst, sem.at[slot])
            cp.start()
            return cp

        def older_blocks(c, _):
            qs = _stack_heads(q_ref[pl.ds(pl.multiple_of(c * QB, QB), QB), :])

            def cond(st):
                kb, live = st
                return jnp.logical_and(kb >= 0, live > -STICK_DEAD)

            def body(st):
                kb, _ = st
                ck = fetch(k_hbm, kbuf, 0, kb)
                cv = fetch(v_hbm, vbuf, 1, kb)
                ck.wait()
                cv.wait()
                carry = carry_scr[c]
                pv, tot = _attn_block(qs, kbuf[...], vbuf[...], tt, carry)
                acc_scr[c] = acc_scr[c] + pv
                carry_scr[c] = carry + tot
                return kb - 1, jnp.max(carry + tot)

            lax.while_loop(cond, body, (i * NQ + c - ATTN_WINDOW, jnp.max(carry_scr[c])))
            return 0

        @pl.when(live_any > -STICK_DEAD)
        def _():
            lax.fori_loop(0, NQ, older_blocks, 0)
            for c in range(NQ):
                yc_scr[c * QB:(c + 1) * QB, :] = (_rms(acc_scr[c]) * gc_ref[...]).astype(BF16)

    @pl.when(i == 0)
    def _():
        step(True, False)

    @pl.when(jnp.logical_and(i > 0, i < n_tiles))
    def _():
        step(True, True)

    @pl.when(i == n_tiles)
    def _():
        step(False, True)


def _attn_ffn(l, x, y_loc, q, k, v, w_out, g_ffn, w_up, w_dn, g_fin, g_att, final_norm):
    b, s, d = x.shape
    g = D_GROUP
    nt = s // TM
    lag = lambda bi, i: (bi, jnp.maximum(i - 1, 0), 0)
    cur = lambda bi, i: (bi, jnp.minimum(i, nt - 1), 0)
    prev = lambda bi, i: (bi, jnp.maximum(jnp.minimum(i, nt - 1) - 1, 0), 0)
    const = lambda bi, i: (0, 0)
    resident = lambda w: pl.BlockSpec(memory_space=pl.ANY)
    return pl.pallas_call(
        functools.partial(_attn_ffn_kernel, layer=l, n_tiles=nt, final_norm=final_norm),
        grid=(b, nt + 1),
        in_specs=[
            pl.BlockSpec((None, TM, d), lag),
            pl.BlockSpec((None, TM, 3 * g), lag),
            pl.BlockSpec((None, TM, g), cur),
            pl.BlockSpec((None, TM, g), cur),
            pl.BlockSpec((None, TM, g), cur),
            pl.BlockSpec((None, TM, g), prev),
            pl.BlockSpec((None, TM, g), prev),
            pl.BlockSpec(memory_space=pl.ANY),
            pl.BlockSpec(memory_space=pl.ANY),
            resident(w_out),
            pl.BlockSpec((1, d), const),
            resident(w_up),
            resident(w_dn),
            pl.BlockSpec((1, d), const),
            pl.BlockSpec((1, g), const),
        ],
        out_specs=pl.BlockSpec((None, TM, d), lag),
        out_shape=jax.ShapeDtypeStruct((b, s, d), F32),
        scratch_shapes=[
            pltpu.VMEM((TM, g), BF16),
            pltpu.VMEM((NQ, N_HEADS * QB, QB), F32),
            pltpu.VMEM((NQ, QB, g), F32),
            pltpu.VMEM((QB, g), BF16),
            pltpu.VMEM((QB, g), BF16),
            pltpu.SemaphoreType.DMA((2,)),
            pltpu.VMEM(w_out.shape[1:], BF16),
            pltpu.VMEM(w_up.shape[1:], BF16),
            pltpu.VMEM(w_dn.shape[1:], BF16),
            pltpu.VMEM((2, WEIGHT_STAGE_ROWS, d), F32),
            pltpu.VMEM((2, WEIGHT_STAGE_ROWS * d // w_up.shape[2], w_up.shape[2]), F32),
            pltpu.SemaphoreType.DMA((4,)),
        ],
        compiler_params=pltpu.CompilerParams(
            dimension_semantics=("arbitrary", "arbitrary"), vmem_limit_bytes=56 * MIB),
        name="attn_ffn",
    )(x, y_loc, q, k, v, k, v, k, v, w_out, g_ffn, w_up, w_dn, g_fin, g_att)


def kernel(x, norm_mix_g, w_in, gmlp_v_g, gmlp_w_s, gmlp_b_s, short_conv_w, conf_conv_w, conf_ln_g, conf_ln_b,
           mix_out_g, w_out, norm_ffn_g, w_up, w_down, final_norm_g):
    depth = w_in.shape[0]
    g = D_GROUP
    for l in range(depth):
        ws_cat = jnp.transpose(gmlp_w_s[l], (1, 0, 2)).reshape(CHUNK, N_HEADS * CHUNK)
        bs_lanes = jnp.repeat(gmlp_b_s[l].T, HEAD_DIM, axis=1)
        go = mix_out_g[l]
        g_loc = jnp.concatenate([go[0:2 * g], go[3 * g:4 * g]])[None, :]
        y_loc, q, k, v = _proj_local(
            x, norm_mix_g[l][None, :], w_in[l].astype(BF16), gmlp_v_g[l][None, :], ws_cat, bs_lanes,
            short_conv_w[l], conf_conv_w[l], conf_ln_g[l][None, :], conf_ln_b[l][None, :], g_loc)
        x = _attn_ffn(
            l, x, y_loc, q, k, v, w_out, norm_ffn_g[l][None, :], w_up, w_down,
            final_norm_g[None, :], go[2 * g:3 * g][None, :], final_norm=(l == depth - 1))
    return x
```

```python
import functools
import math

import jax
import jax.numpy as jnp
from jax import lax
from jax.experimental import pallas as pl
from jax.experimental.pallas import tpu as pltpu

D_MODEL = 1024
D_GROUP = 256
HEAD_DIM = 64
N_HEADS = D_GROUP // HEAD_DIM
CHUNK = 128
SHORT_W = 3
CONF_W = 31
D_FF = 4 * D_MODEL
D_IN_PROJ = 10 * D_GROUP
EPS = 1e-6

F32 = jnp.float32
BF16 = jnp.bfloat16

SUBLANES = 8
MIB = 1024 * 1024

TM = 512
TMP = 1024
SUB = 512
QB = CHUNK
NQ = TM // QB
ATTN_WINDOW = 3
WEIGHT_STAGE_ROWS = 512
SHORT_HALO = SUBLANES
CONF_HALO = 4 * SUBLANES
STICK_DEAD = 110.0


def _rms(x):
    return x * lax.rsqrt(jnp.mean(x * x, axis=-1, keepdims=True) + EPS)


def _gelu_tanh(x):
    c = math.sqrt(2.0 / math.pi)
    return 0.5 * x * (1.0 + jnp.tanh(c * (x + 0.044715 * (x * x * x))))


def _head_of_lane():
    return lax.broadcasted_iota(jnp.int32, (QB, D_GROUP), 1) // HEAD_DIM


def _block_diag(vblk):
    hol = _head_of_lane()
    return [jnp.where(hol == h, vblk, jnp.zeros_like(vblk)) for h in range(N_HEADS)]


def _local_chunk(zr, s0, za, zb, pbuf, hbuf, wcat, vg_ref, bs_ref, wsc_ref, wcc_ref, lng_ref, lnb_ref):
    g = D_GROUP
    r0 = s0 + zr
    ga = _gelu_tanh(za[zr:zr + CHUNK, :])
    u = ga[:, 0:g]
    v = (_rms(ga[:, g:2 * g]) * vg_ref[...]).astype(BF16)
    f = jnp.dot(wcat, jnp.concatenate(_block_diag(v), axis=0), preferred_element_type=F32) + bs_ref[...]
    ya = u * f
    conv_b = wsc_ref[SHORT_W - 1:SHORT_W, :] * pbuf[SHORT_HALO + r0:SHORT_HALO + r0 + CHUNK, :]
    for kk in range(SHORT_W - 1):
        off = SHORT_HALO + r0 - (SHORT_W - 1) + kk
        conv_b = conv_b + wsc_ref[kk:kk + 1, :] * pbuf[off:off + CHUNK, :]
    yb = zb[zr:zr + CHUNK, 0:g] * conv_b
    base = CONF_HALO + r0 - (CONF_W - 1)
    conv_d = None
    for res in range(SUBLANES):
        part = None
        for kk in range(CONF_W):
            if (base + kk) % SUBLANES != res:
                continue
            off = base + kk - res
            term = wcc_ref[kk:kk + 1, :] * hbuf[off:off + CHUNK + SUBLANES, :]
            part = term if part is None else part + term
        if part is None:
            continue
        shifted = part[res:res + CHUNK, :]
        conv_d = shifted if conv_d is None else conv_d + shifted
    xc = conv_d - jnp.mean(conv_d, axis=-1, keepdims=True)
    var = jnp.mean(xc * xc, axis=-1, keepdims=True)
    ln = xc * lax.rsqrt(var + EPS) * lng_ref[...] + lnb_ref[...]
    yd = ln * jax.nn.sigmoid(ln)
    return ya, yb, yd


def _proj_local_kernel(x_ref, gin_ref, win_ref, vg_ref, ws_ref, bs_ref, wsc_ref, wcc_ref, lng_ref, lnb_ref,
                       go_ref, yl_ref, q_ref, k_ref, v_ref, pbuf, hbuf):
    g = D_GROUP
    t = TMP

    @pl.when(pl.program_id(1) == 0)
    def _():
        pbuf[0:SHORT_HALO, :] = jnp.zeros((SHORT_HALO, g), F32)
        hbuf[...] = jnp.zeros(hbuf.shape, F32)

    rr = lax.broadcasted_iota(jnp.int32, (CHUNK, N_HEADS * CHUNK), 0)
    cc = lax.broadcasted_iota(jnp.int32, (CHUNK, N_HEADS * CHUNK), 1)
    wcat = jnp.where((cc % CHUNK) <= rr, ws_ref[...], 0.0).astype(BF16)

    nsub = TMP // SUB

    def proj(sub, lo, hi):
        return jnp.dot(hs[sub], win_ref[:, lo:hi], preferred_element_type=F32)

    hs, zas, zbs = [], [], []
    for sub in range(nsub):
        s0 = sub * SUB
        hs.append((_rms(x_ref[s0:s0 + SUB, :]) * gin_ref[...]).astype(BF16))
        zd = proj(sub, 8 * g, 10 * g)
        hbuf[CONF_HALO + s0:CONF_HALO + s0 + SUB, :] = zd[:, 0:g] * jax.nn.sigmoid(zd[:, g:2 * g])
    for sub in range(nsub):
        s0 = sub * SUB
        zb = proj(sub, 2 * g, 5 * g)
        zbs.append(zb)
        pbuf[SHORT_HALO + s0:SHORT_HALO + s0 + SUB, :] = zb[:, g:2 * g] * zb[:, 2 * g:3 * g]
    for sub in range(nsub):
        zas.append(proj(sub, 0, 2 * g))
    for sub in range(nsub):
        s0 = sub * SUB
        for cs in range(SUB // CHUNK):
            r0 = s0 + cs * CHUNK
            ya, yb, yd = _local_chunk(cs * CHUNK, s0, zas[sub], zbs[sub], pbuf, hbuf, wcat,
                                      vg_ref, bs_ref, wsc_ref, wcc_ref, lng_ref, lnb_ref)
            yl_ref[r0:r0 + CHUNK, 0:g] = (_rms(ya) * go_ref[:, 0:g]).astype(BF16)
            yl_ref[r0:r0 + CHUNK, g:2 * g] = (_rms(yb) * go_ref[:, g:2 * g]).astype(BF16)
            yl_ref[r0:r0 + CHUNK, 2 * g:3 * g] = (_rms(yd) * go_ref[:, 2 * g:3 * g]).astype(BF16)
    for sub in range(nsub):
        s0 = sub * SUB
        qkv = proj(sub, 5 * g, 8 * g)
        q_ref[s0:s0 + SUB, :] = (qkv[:, 0:g] * -(HEAD_DIM ** -0.5)).astype(BF16)
        k_ref[s0:s0 + SUB, :] = qkv[:, g:2 * g].astype(BF16)
        v_ref[s0:s0 + SUB, :] = qkv[:, 2 * g:3 * g].astype(BF16)

    pbuf[0:SHORT_HALO, :] = pbuf[t:t + SHORT_HALO, :]
    hbuf[0:CONF_HALO, :] = hbuf[t:t + CONF_HALO, :]


def _proj_local(x, g_in, w_in, vg, ws_cat, bs_lanes, w_short, w_conf, ln_g, ln_b, g_loc):
    b, s, d = x.shape
    g = D_GROUP
    tile = lambda bi, i: (bi, i, 0)
    const = lambda bi, i: (0, 0)
    return pl.pallas_call(
        _proj_local_kernel,
        grid=(b, s // TMP),
        in_specs=[
            pl.BlockSpec((None, TMP, d), tile),
            pl.BlockSpec((1, d), const),
            pl.BlockSpec((d, D_IN_PROJ), const, pipeline_mode=pl.Buffered(1)),
            pl.BlockSpec((1, g), const),
            pl.BlockSpec((CHUNK, N_HEADS * CHUNK), const),
            pl.BlockSpec((CHUNK, g), const),
            pl.BlockSpec((SHORT_W, g), const),
            pl.BlockSpec((CONF_W, g), const),
            pl.BlockSpec((1, g), const),
            pl.BlockSpec((1, g), const),
            pl.BlockSpec((1, 3 * g), const),
        ],
        out_specs=[
            pl.BlockSpec((None, TMP, 3 * g), tile),
            pl.BlockSpec((None, TMP, g), tile),
            pl.BlockSpec((None, TMP, g), tile),
            pl.BlockSpec((None, TMP, g), tile),
        ],
        out_shape=[
            jax.ShapeDtypeStruct((b, s, 3 * g), BF16),
            jax.ShapeDtypeStruct((b, s, g), BF16),
            jax.ShapeDtypeStruct((b, s, g), BF16),
            jax.ShapeDtypeStruct((b, s, g), BF16),
        ],
        scratch_shapes=[
            pltpu.VMEM((SHORT_HALO + TMP, g), F32),
            pltpu.VMEM((CONF_HALO + TMP + SUBLANES, g), F32),
        ],
        compiler_params=pltpu.CompilerParams(
            dimension_semantics=("arbitrary", "arbitrary"), vmem_limit_bytes=40 * MIB),
        name="proj_local",
    )(x, g_in, w_in, vg, ws_cat, bs_lanes, w_short, w_conf, ln_g, ln_b, g_loc)


def _stack_heads(q):
    return jnp.concatenate(_block_diag(q), axis=0)


def _suffix_matrix():
    rr = lax.broadcasted_iota(jnp.int32, (2 * QB, 2 * QB), 0) % QB
    cc = lax.broadcasted_iota(jnp.int32, (2 * QB, 2 * QB), 1)
    return jnp.where((cc >= QB) | (rr > cc), 1.0, 0.0).astype(BF16)


def _neg_softplus(ns):
    return jnp.minimum(ns, 0.0) - jnp.log(1.0 + jnp.exp2(jnp.abs(ns) * (-math.log2(math.e))))


def _split(lom):
    hi = lom.astype(BF16)
    lo = (lom - hi.astype(F32)).astype(BF16)
    return jnp.concatenate([hi, lo], axis=1)


def _heads_to_lanes(wb):
    return [wb[h * QB:(h + 1) * QB, :] for h in range(N_HEADS)]


class _Window:
    def __init__(self, q, kblks, vblks, exists, tt):
        self.q, self.kblks, self.vblks, self.exists, self.tt = q, kblks, vblks, exists, tt
        self.nw = len(kblks)
        nh = N_HEADS
        qrow = lax.broadcasted_iota(jnp.int32, (nh * QB, QB), 0) % QB
        kcol = lax.broadcasted_iota(jnp.int32, (nh * QB, QB), 1)
        self.causal = kcol < qrow

    def _mask(self, j, val):
        if j == self.nw - 1:
            return jnp.where(self.causal, val, 0.0)
        return val if self.exists[j] is None else jnp.where(self.exists[j], val, 0.0)

    def scores(self):
        self.ns = lax.dot_general(_stack_heads(self.q), jnp.concatenate(self.kblks, axis=0),
                                  (((1,), (1,)), ((), ())), preferred_element_type=F32)
        return self.ns

    def suffix_sums(self):
        nw = self.nw
        self.ns_j = [self.ns[:, j * QB:(j + 1) * QB] for j in range(nw)]
        self.lom_j = [self._mask(j, _neg_softplus(self.ns_j[j])) for j in range(nw)]
        self.ll = jnp.dot(jnp.concatenate([_split(l) for l in self.lom_j], axis=0), self.tt,
                          preferred_element_type=F32)
        return self.ll

    def weights(self):
        nw, nh = self.nw, N_HEADS
        later = None
        wcols = [None] * nw
        for j in reversed(range(nw)):
            rows = slice(j * nh * QB, (j + 1) * nh * QB)
            logw = (self.lom_j[j] - self.ns_j[j]) + self.ll[rows, 0:QB]
            if later is not None:
                logw = logw + later
            wcols[j] = _heads_to_lanes(self._mask(j, jnp.exp(logw)).astype(BF16))
            tot = self.ll[rows, QB:2 * QB]
            later = tot if later is None else later + tot
        self.wcat = jnp.concatenate([c for cols in wcols for c in cols], axis=1)
        return later

    def values(self):
        vbd = jnp.concatenate([m for vb in self.vblks for m in _block_diag(vb)], axis=0)
        return jnp.dot(self.wcat, vbd, preferred_element_type=F32)


def _after(x, vals):
    z = sum(jnp.minimum(jnp.maximum(v[0:1, 0:CHUNK], 0.0), 0.0) for v in vals)
    return jnp.concatenate([x[:, 0:CHUNK] + z.astype(x.dtype), x[:, CHUNK:]], axis=1)


def _attn_block(qs, kblk, vblk, tt, carry):
    nsb = lax.dot_general(qs, kblk, (((1,), (1,)), ((), ())), preferred_element_type=F32)
    lom = _neg_softplus(nsb)
    llb = jnp.dot(_split(lom), tt, preferred_element_type=F32)
    w = jnp.exp((lom - nsb) + llb[:, 0:QB] + carry)
    wcat = jnp.concatenate(_heads_to_lanes(w.astype(BF16)), axis=1)
    vbd = jnp.concatenate(_block_diag(vblk), axis=0)
    return jnp.dot(wcat, vbd, preferred_element_type=F32), llb[:, QB:2 * QB]


def _attn_ffn_kernel(x_ref, yl_ref, q_ref, kc_ref, vc_ref, kp_ref, vp_ref, k_hbm, v_hbm,
                     wo_hbm, gf_ref, wup_hbm, wdn_hbm, gfin_ref, gc_ref,
                     o_ref, yc_scr, carry_scr, acc_scr, kbuf, vbuf, sem,
                     wo_ref, wup_ref, wdn_ref, stage_sq, stage_up, wsem, *, layer, n_tiles, final_norm):
    bi = pl.program_id(0)
    i = pl.program_id(1)
    g = D_GROUP

    @pl.when(jnp.logical_and(bi == 0, i == 0))
    def _():
        rows_sq, rows_up = stage_sq.shape[1], stage_up.shape[1]
        jobs = []
        for src, dst, stage, s0, rows in ((wo_hbm, wo_ref, stage_sq, 0, rows_sq), (wdn_hbm, wdn_ref, stage_sq, 0, rows_sq),
                                          (wup_hbm, wup_ref, stage_up, 2, rows_up)):
            for r in range(dst.shape[0] // rows):
                jobs.append((src.at[layer, pl.ds(r * rows, rows), :], stage, s0, dst, r * rows, rows))
        copies = []
        for n, (src, stage, s0, _, _, _) in enumerate(jobs):
            copies.append(pltpu.make_async_copy(src, stage.at[n % 2], wsem.at[s0 + n % 2]))
        copies[0].start()
        for n, (_, stage, _, dst, r0, rows) in enumerate(jobs):
            if n + 1 < len(jobs):
                copies[n + 1].start()
            copies[n].wait()
            dst[r0:r0 + rows, :] = stage[n % 2].astype(BF16)

    def step(do_attn, do_mlp):
        if do_attn:
            tt = _suffix_matrix()
            has_prev = i > 0
            wins = []
            for c in range(NQ):
                kblks, vblks, exists = [], [], []
                for j in range(ATTN_WINDOW):
                    rel = c - (ATTN_WINDOW - 1) + j
                    kr, vr, r = (kc_ref, vc_ref, rel) if rel >= 0 else (kp_ref, vp_ref, NQ + rel)
                    kblks.append(kr[r * QB:(r + 1) * QB, :])
                    vblks.append(vr[r * QB:(r + 1) * QB, :])
                    exists.append(None if rel >= 0 else has_prev)
                wins.append(_Window(q_ref[c * QB:(c + 1) * QB, :], kblks, vblks, exists, tt))

        order = _after if (do_attn and do_mlp) else (lambda val, _deps: val)
        if do_attn:
            ns_all = [w.scores() for w in wins]
        if do_mlp:
            yl = order(yl_ref[...], ns_all if do_attn else None)
            y = jnp.concatenate([yl[:, 0:2 * g], yc_scr[...], yl[:, 2 * g:3 * g]], axis=1)
            x1 = x_ref[...] + jnp.dot(y, wo_ref[...], preferred_element_type=F32)
        if do_attn:
            ll_all = [w.suffix_sums() for w in wins]
        if do_mlp:
            h = order(_rms(x1) * gf_ref[...], ll_all if do_attn else None).astype(BF16)
            a = jnp.maximum(jnp.dot(h, wup_ref[...], preferred_element_type=F32), 0.0)
        if do_attn:
            carries = [w.weights() for w in wins]
            for c in range(NQ):
                carry_scr[c] = carries[c]
            live_any = jnp.max(functools.reduce(jnp.maximum, carries))
            pv_all = [w.values() for w in wins]
            for c in range(NQ):
                acc_scr[c] = pv_all[c]
                yc_scr[c * QB:(c + 1) * QB, :] = (_rms(pv_all[c]) * gc_ref[...]).astype(BF16)
        if do_mlp:
            a2 = order(a * a, pv_all if do_attn else None).astype(BF16)
            x2 = x1 + jnp.dot(a2, wdn_ref[...], preferred_element_type=F32)
            if final_norm:
                x2 = _rms(x2) * gfin_ref[...]
            o_ref[...] = x2
        if not do_attn:
            return

        def fetch(src, dst, slot, kb):
            cp = pltpu.make_async_copy(src.at[bi, pl.ds(pl.multiple_of(kb * QB, QB), QB), :], dst, sem.at[slot])
            cp.start()
            return cp

        def older_blocks(c, _):
            qs = _stack_heads(q_ref[pl.ds(pl.multiple_of(c * QB, QB), QB), :])

            def cond(st):
                kb, live = st
                return jnp.logical_and(kb >= 0, live > -STICK_DEAD)

            def body(st):
                kb, _ = st
                ck = fetch(k_hbm, kbuf, 0, kb)
                cv = fetch(v_hbm, vbuf, 1, kb)
                ck.wait()
                cv.wait()
                carry = carry_scr[c]
                pv, tot = _attn_block(qs, kbuf[...], vbuf[...], tt, carry)
                acc_scr[c] = acc_scr[c] + pv
                carry_scr[c] = carry + tot
                return kb - 1, jnp.max(carry + tot)

            lax.while_loop(cond, body, (i * NQ + c - ATTN_WINDOW, jnp.max(carry_scr[c])))
            return 0

        @pl.when(live_any > -STICK_DEAD)
        def _():
            lax.fori_loop(0, NQ, older_blocks, 0)
            for c in range(NQ):
                yc_scr[c * QB:(c + 1) * QB, :] = (_rms(acc_scr[c]) * gc_ref[...]).astype(BF16)

    @pl.when(i == 0)
    def _():
        step(True, False)

    @pl.when(jnp.logical_and(i > 0, i < n_tiles))
    def _():
        step(True, True)

    @pl.when(i == n_tiles)
    def _():
        step(False, True)


def _attn_ffn(l, x, y_loc, q, k, v, w_out, g_ffn, w_up, w_dn, g_fin, g_att, final_norm):
    b, s, d = x.shape
    g = D_GROUP
    nt = s // TM
    lag = lambda bi, i: (bi, jnp.maximum(i - 1, 0), 0)
    cur = lambda bi, i: (bi, jnp.minimum(i, nt - 1), 0)
    prev = lambda bi, i: (bi, jnp.maximum(jnp.minimum(i, nt - 1) - 1, 0), 0)
    const = lambda bi, i: (0, 0)
    resident = lambda w: pl.BlockSpec(memory_space=pl.ANY)
    return pl.pallas_call(
        functools.partial(_attn_ffn_kernel, layer=l, n_tiles=nt, final_norm=final_norm),
        grid=(b, nt + 1),
        in_specs=[
            pl.BlockSpec((None, TM, d), lag),
            pl.BlockSpec((None, TM, 3 * g), lag),
            pl.BlockSpec((None, TM, g), cur),
            pl.BlockSpec((None, TM, g), cur),
            pl.BlockSpec((None, TM, g), cur),
            pl.BlockSpec((None, TM, g), prev),
            pl.BlockSpec((None, TM, g), prev),
            pl.BlockSpec(memory_space=pl.ANY),
            pl.BlockSpec(memory_space=pl.ANY),
            resident(w_out),
            pl.BlockSpec((1, d), const),
            resident(w_up),
            resident(w_dn),
            pl.BlockSpec((1, d), const),
            pl.BlockSpec((1, g), const),
        ],
        out_specs=pl.BlockSpec((None, TM, d), lag),
        out_shape=jax.ShapeDtypeStruct((b, s, d), F32),
        scratch_shapes=[
            pltpu.VMEM((TM, g), BF16),
            pltpu.VMEM((NQ, N_HEADS * QB, QB), F32),
            pltpu.VMEM((NQ, QB, g), F32),
            pltpu.VMEM((QB, g), BF16),
            pltpu.VMEM((QB, g), BF16),
            pltpu.SemaphoreType.DMA((2,)),
            pltpu.VMEM(w_out.shape[1:], BF16),
            pltpu.VMEM(w_up.shape[1:], BF16),
            pltpu.VMEM(w_dn.shape[1:], BF16),
            pltpu.VMEM((2, WEIGHT_STAGE_ROWS, d), F32),
            pltpu.VMEM((2, WEIGHT_STAGE_ROWS * d // w_up.shape[2], w_up.shape[2]), F32),
            pltpu.SemaphoreType.DMA((4,)),
        ],
        compiler_params=pltpu.CompilerParams(
            dimension_semantics=("arbitrary", "arbitrary"), vmem_limit_bytes=56 * MIB),
        name="attn_ffn",
    )(x, y_loc, q, k, v, k, v, k, v, w_out, g_ffn, w_up, w_dn, g_fin, g_att)


def kernel(x, norm_mix_g, w_in, gmlp_v_g, gmlp_w_s, gmlp_b_s, short_conv_w, conf_conv_w, conf_ln_g, conf_ln_b,
           mix_out_g, w_out, norm_ffn_g, w_up, w_down, final_norm_g):
    depth = w_in.shape[0]
    g = D_GROUP
    for l in range(depth):
        ws_cat = jnp.transpose(gmlp_w_s[l], (1, 0, 2)).reshape(CHUNK, N_HEADS * CHUNK)
        bs_lanes = jnp.repeat(gmlp_b_s[l].T, HEAD_DIM, axis=1)
        go = mix_out_g[l]
        g_loc = jnp.concatenate([go[0:2 * g], go[3 * g:4 * g]])[None, :]
        y_loc, q, k, v = _proj_local(
            x, norm_mix_g[l][None, :], w_in[l].astype(BF16), gmlp_v_g[l][None, :], ws_cat, bs_lanes,
            short_conv_w[l], conf_conv_w[l], conf_ln_g[l][None, :], conf_ln_b[l][None, :], g_loc)
        x = _attn_ffn(
            l, x, y_loc, q, k, v, w_out, norm_ffn_g[l][None, :], w_up, w_down,
            final_norm_g[None, :], go[2 * g:3 * g][None, :], final_norm=(l == depth - 1))
    return x
```

```python
import functools
import math

import jax
import jax.numpy as jnp
from jax import lax
from jax.experimental import pallas as pl
from jax.experimental.pallas import tpu as pltpu

D_MODEL = 1024
D_GROUP = 256
HEAD_DIM = 64
N_HEADS = D_GROUP // HEAD_DIM
CHUNK = 128
SHORT_W = 3
CONF_W = 31
D_FF = 4 * D_MODEL
D_IN_PROJ = 10 * D_GROUP
EPS = 1e-6

F32 = jnp.float32
BF16 = jnp.bfloat16

SUBLANES = 8
MIB = 1024 * 1024

TM = 512
TMP = 1024
SUB = 512
QB = CHUNK
NQ = TM // QB
ATTN_WINDOW = 3
WEIGHT_STAGE_ROWS = 512
SHORT_HALO = SUBLANES
CONF_HALO = 4 * SUBLANES
STICK_DEAD = 110.0


def _rms(x):
    return x * lax.rsqrt(jnp.mean(x * x, axis=-1, keepdims=True) + EPS)


def _gelu_tanh(x):
    c = math.sqrt(2.0 / math.pi)
    return 0.5 * x * (1.0 + jnp.tanh(c * (x + 0.044715 * (x * x * x))))


def _head_of_lane():
    return lax.broadcasted_iota(jnp.int32, (QB, D_GROUP), 1) // HEAD_DIM


def _block_diag(vblk):
    hol = _head_of_lane()
    return [jnp.where(hol == h, vblk, jnp.zeros_like(vblk)) for h in range(N_HEADS)]


def _local_chunk(zr, s0, za, zb, pbuf, hbuf, wcat, vg_ref, bs_ref, wsc_ref, wcc_ref, lng_ref, lnb_ref):
    g = D_GROUP
    r0 = s0 + zr
    ga = _gelu_tanh(za[zr:zr + CHUNK, :])
    u = ga[:, 0:g]
    v = (_rms(ga[:, g:2 * g]) * vg_ref[...]).astype(BF16)
    f = jnp.dot(wcat, jnp.concatenate(_block_diag(v), axis=0), preferred_element_type=F32) + bs_ref[...]
    ya = u * f
    conv_b = wsc_ref[SHORT_W - 1:SHORT_W, :] * pbuf[SHORT_HALO + r0:SHORT_HALO + r0 + CHUNK, :]
    for kk in range(SHORT_W - 1):
        off = SHORT_HALO + r0 - (SHORT_W - 1) + kk
        conv_b = conv_b + wsc_ref[kk:kk + 1, :] * pbuf[off:off + CHUNK, :]
    yb = zb[zr:zr + CHUNK, 0:g] * conv_b
    base = CONF_HALO + r0 - (CONF_W - 1)
    conv_d = None
    for res in range(SUBLANES):
        part = None
        for kk in range(CONF_W):
            if (base + kk) % SUBLANES != res:
                continue
            off = base + kk - res
            term = wcc_ref[kk:kk + 1, :] * hbuf[off:off + CHUNK + SUBLANES, :]
            part = term if part is None else part + term
        if part is None:
            continue
        shifted = part[res:res + CHUNK, :]
        conv_d = shifted if conv_d is None else conv_d + shifted
    xc = conv_d - jnp.mean(conv_d, axis=-1, keepdims=True)
    var = jnp.mean(xc * xc, axis=-1, keepdims=True)
    ln = xc * lax.rsqrt(var + EPS) * lng_ref[...] + lnb_ref[...]
    yd = ln * jax.nn.sigmoid(ln)
    return ya, yb, yd


def _proj_local_kernel(x_ref, gin_ref, win_hbm, vg_ref, ws_ref, bs_ref, wsc_ref, wcc_ref, lng_ref, lnb_ref,
                       go_ref, yl_ref, q_ref, k_ref, v_ref, pbuf, hbuf, win_ref, wstage, wsem, *, layer):
    @pl.when(jnp.logical_and(pl.program_id(0) == 0, pl.program_id(1) == 0))
    def _():
        rows = wstage.shape[1]
        n_jobs = win_ref.shape[0] // rows
        copies = [pltpu.make_async_copy(win_hbm.at[layer, pl.ds(n * rows, rows), :], wstage.at[n % 2],
                                        wsem.at[n % 2]) for n in range(n_jobs)]
        copies[0].start()
        for n in range(n_jobs):
            if n + 1 < n_jobs:
                copies[n + 1].start()
            copies[n].wait()
            win_ref[n * rows:(n + 1) * rows, :] = wstage[n % 2].astype(BF16)

    g = D_GROUP
    t = TMP

    @pl.when(pl.program_id(1) == 0)
    def _():
        pbuf[0:SHORT_HALO, :] = jnp.zeros((SHORT_HALO, g), F32)
        hbuf[...] = jnp.zeros(hbuf.shape, F32)

    rr = lax.broadcasted_iota(jnp.int32, (CHUNK, N_HEADS * CHUNK), 0)
    cc = lax.broadcasted_iota(jnp.int32, (CHUNK, N_HEADS * CHUNK), 1)
    wcat = jnp.where((cc % CHUNK) <= rr, ws_ref[...], 0.0).astype(BF16)

    nsub = TMP // SUB

    def proj(sub, lo, hi):
        return jnp.dot(hs[sub], win_ref[:, lo:hi], preferred_element_type=F32)

    hs, zas, zbs = [], [], []
    for sub in range(nsub):
        s0 = sub * SUB
        hs.append((_rms(x_ref[s0:s0 + SUB, :]) * gin_ref[...]).astype(BF16))
        zd = proj(sub, 8 * g, 10 * g)
        hbuf[CONF_HALO + s0:CONF_HALO + s0 + SUB, :] = zd[:, 0:g] * jax.nn.sigmoid(zd[:, g:2 * g])
    for sub in range(nsub):
        s0 = sub * SUB
        zb = proj(sub, 2 * g, 5 * g)
        zbs.append(zb)
        pbuf[SHORT_HALO + s0:SHORT_HALO + s0 + SUB, :] = zb[:, g:2 * g] * zb[:, 2 * g:3 * g]
    for sub in range(nsub):
        zas.append(proj(sub, 0, 2 * g))
    for sub in range(nsub):
        s0 = sub * SUB
        for cs in range(SUB // CHUNK):
            r0 = s0 + cs * CHUNK
            ya, yb, yd = _local_chunk(cs * CHUNK, s0, zas[sub], zbs[sub], pbuf, hbuf, wcat,
                                      vg_ref, bs_ref, wsc_ref, wcc_ref, lng_ref, lnb_ref)
            yl_ref[r0:r0 + CHUNK, 0:g] = (_rms(ya) * go_ref[:, 0:g]).astype(BF16)
            yl_ref[r0:r0 + CHUNK, g:2 * g] = (_rms(yb) * go_ref[:, g:2 * g]).astype(BF16)
            yl_ref[r0:r0 + CHUNK, 2 * g:3 * g] = (_rms(yd) * go_ref[:, 2 * g:3 * g]).astype(BF16)
    for sub in range(nsub):
        s0 = sub * SUB
        qkv = proj(sub, 5 * g, 8 * g)
        q_ref[s0:s0 + SUB, :] = (qkv[:, 0:g] * -(HEAD_DIM ** -0.5)).astype(BF16)
        k_ref[s0:s0 + SUB, :] = qkv[:, g:2 * g].astype(BF16)
        v_ref[s0:s0 + SUB, :] = qkv[:, 2 * g:3 * g].astype(BF16)

    pbuf[0:SHORT_HALO, :] = pbuf[t:t + SHORT_HALO, :]
    hbuf[0:CONF_HALO, :] = hbuf[t:t + CONF_HALO, :]


def _proj_local(l, x, g_in, w_in, vg, ws_cat, bs_lanes, w_short, w_conf, ln_g, ln_b, g_loc):
    b, s, d = x.shape
    g = D_GROUP
    tile = lambda bi, i: (bi, i, 0)
    const = lambda bi, i: (0, 0)
    return pl.pallas_call(
        functools.partial(_proj_local_kernel, layer=l),
        grid=(b, s // TMP),
        in_specs=[
            pl.BlockSpec((None, TMP, d), tile),
            pl.BlockSpec((1, d), const),
            pl.BlockSpec(memory_space=pl.ANY),
            pl.BlockSpec((1, g), const),
            pl.BlockSpec((CHUNK, N_HEADS * CHUNK), const),
            pl.BlockSpec((CHUNK, g), const),
            pl.BlockSpec((SHORT_W, g), const),
            pl.BlockSpec((CONF_W, g), const),
            pl.BlockSpec((1, g), const),
            pl.BlockSpec((1, g), const),
            pl.BlockSpec((1, 3 * g), const),
        ],
        out_specs=[
            pl.BlockSpec((None, TMP, 3 * g), tile),
            pl.BlockSpec((None, TMP, g), tile),
            pl.BlockSpec((None, TMP, g), tile),
            pl.BlockSpec((None, TMP, g), tile),
        ],
        out_shape=[
            jax.ShapeDtypeStruct((b, s, 3 * g), BF16),
            jax.ShapeDtypeStruct((b, s, g), BF16),
            jax.ShapeDtypeStruct((b, s, g), BF16),
            jax.ShapeDtypeStruct((b, s, g), BF16),
        ],
        scratch_shapes=[
            pltpu.VMEM((SHORT_HALO + TMP, g), F32),
            pltpu.VMEM((CONF_HALO + TMP + SUBLANES, g), F32),
            pltpu.VMEM((d, D_IN_PROJ), BF16),
            pltpu.VMEM((2, CHUNK, D_IN_PROJ), F32),
            pltpu.SemaphoreType.DMA((2,)),
        ],
        compiler_params=pltpu.CompilerParams(
            dimension_semantics=("arbitrary", "arbitrary"), vmem_limit_bytes=40 * MIB),
        name="proj_local",
    )(x, g_in, w_in, vg, ws_cat, bs_lanes, w_short, w_conf, ln_g, ln_b, g_loc)


def _stack_heads(q):
    return jnp.concatenate(_block_diag(q), axis=0)


def _suffix_matrix():
    rr = lax.broadcasted_iota(jnp.int32, (2 * QB, 2 * QB), 0) % QB
    cc = lax.broadcasted_iota(jnp.int32, (2 * QB, 2 * QB), 1)
    return jnp.where((cc >= QB) | (rr > cc), 1.0, 0.0).astype(BF16)


def _neg_softplus(ns):
    return jnp.minimum(ns, 0.0) - jnp.log(1.0 + jnp.exp2(jnp.abs(ns) * (-math.log2(math.e))))


def _split(lom):
    hi = lom.astype(BF16)
    lo = (lom - hi.astype(F32)).astype(BF16)
    return jnp.concatenate([hi, lo], axis=1)


def _heads_to_lanes(wb):
    return [wb[h * QB:(h + 1) * QB, :] for h in range(N_HEADS)]


class _Window:
    def __init__(self, q, kblks, vblks, exists, tt):
        self.q, self.kblks, self.vblks, self.exists, self.tt = q, kblks, vblks, exists, tt
        self.nw = len(kblks)
        nh = N_HEADS
        qrow = lax.broadcasted_iota(jnp.int32, (nh * QB, QB), 0) % QB
        kcol = lax.broadcasted_iota(jnp.int32, (nh * QB, QB), 1)
        self.causal = kcol < qrow

    def _mask(self, j, val):
        if j == self.nw - 1:
            return jnp.where(self.causal, val, 0.0)
        return val if self.exists[j] is None else jnp.where(self.exists[j], val, 0.0)

    def scores(self):
        self.ns = lax.dot_general(_stack_heads(self.q), jnp.concatenate(self.kblks, axis=0),
                                  (((1,), (1,)), ((), ())), preferred_element_type=F32)
        return self.ns

    def suffix_sums(self):
        nw = self.nw
        self.ns_j = [self.ns[:, j * QB:(j + 1) * QB] for j in range(nw)]
        self.lom_j = [self._mask(j, _neg_softplus(self.ns_j[j])) for j in range(nw)]
        self.ll = jnp.dot(jnp.concatenate([_split(l) for l in self.lom_j], axis=0), self.tt,
                          preferred_element_type=F32)
        return self.ll

    def weights(self):
        nw, nh = self.nw, N_HEADS
        later = None
        wcols = [None] * nw
        for j in reversed(range(nw)):
            rows = slice(j * nh * QB, (j + 1) * nh * QB)
            logw = (self.lom_j[j] - self.ns_j[j]) + self.ll[rows, 0:QB]
            if later is not None:
                logw = logw + later
            wcols[j] = _heads_to_lanes(self._mask(j, jnp.exp(logw)).astype(BF16))
            tot = self.ll[rows, QB:2 * QB]
            later = tot if later is None else later + tot
        self.wcat = jnp.concatenate([c for cols in wcols for c in cols], axis=1)
        return later

    def values(self):
        vbd = jnp.concatenate([m for vb in self.vblks for m in _block_diag(vb)], axis=0)
        return jnp.dot(self.wcat, vbd, preferred_element_type=F32)


def _after(x, vals):
    z = sum(jnp.minimum(jnp.maximum(v[0:1, 0:CHUNK], 0.0), 0.0) for v in vals)
    return jnp.concatenate([x[:, 0:CHUNK] + z.astype(x.dtype), x[:, CHUNK:]], axis=1)


def _attn_block(qs, kblk, vblk, tt, carry):
    nsb = lax.dot_general(qs, kblk, (((1,), (1,)), ((), ())), preferred_element_type=F32)
    lom = _neg_softplus(nsb)
    llb = jnp.dot(_split(lom), tt, preferred_element_type=F32)
    w = jnp.exp((lom - nsb) + llb[:, 0:QB] + carry)
    wcat = jnp.concatenate(_heads_to_lanes(w.astype(BF16)), axis=1)
    vbd = jnp.concatenate(_block_diag(vblk), axis=0)
    return jnp.dot(wcat, vbd, preferred_element_type=F32), llb[:, QB:2 * QB]


def _attn_ffn_kernel(x_ref, yl_ref, q_ref, kc_ref, vc_ref, kp_ref, vp_ref, k_hbm, v_hbm,
                     wo_hbm, gf_ref, wup_hbm, wdn_hbm, gfin_ref, gc_ref,
                     o_ref, yc_scr, carry_scr, acc_scr, kbuf, vbuf, sem,
                     wo_ref, wup_ref, wdn_ref, stage_sq, stage_up, wsem, *, layer, n_tiles, final_norm):
    bi = pl.program_id(0)
    i = pl.program_id(1)
    g = D_GROUP

    @pl.when(jnp.logical_and(bi == 0, i == 0))
    def _():
        rows_sq, rows_up = stage_sq.shape[1], stage_up.shape[1]
        jobs = []
        for src, dst, stage, s0, rows in ((wo_hbm, wo_ref, stage_sq, 0, rows_sq), (wdn_hbm, wdn_ref, stage_sq, 0, rows_sq),
                                          (wup_hbm, wup_ref, stage_up, 2, rows_up)):
            for r in range(dst.shape[0] // rows):
                jobs.append((src.at[layer, pl.ds(r * rows, rows), :], stage, s0, dst, r * rows, rows))
        copies = []
        for n, (src, stage, s0, _, _, _) in enumerate(jobs):
            copies.append(pltpu.make_async_copy(src, stage.at[n % 2], wsem.at[s0 + n % 2]))
        copies[0].start()
        for n, (_, stage, _, dst, r0, rows) in enumerate(jobs):
            if n + 1 < len(jobs):
                copies[n + 1].start()
            copies[n].wait()
            dst[r0:r0 + rows, :] = stage[n % 2].astype(BF16)

    def step(do_attn, do_mlp):
        if do_attn:
            tt = _suffix_matrix()
            has_prev = i > 0
            wins = []
            for c in range(NQ):
                kblks, vblks, exists = [], [], []
                for j in range(ATTN_WINDOW):
                    rel = c - (ATTN_WINDOW - 1) + j
                    kr, vr, r = (kc_ref, vc_ref, rel) if rel >= 0 else (kp_ref, vp_ref, NQ + rel)
                    kblks.append(kr[r * QB:(r + 1) * QB, :])
                    vblks.append(vr[r * QB:(r + 1) * QB, :])
                    exists.append(None if rel >= 0 else has_prev)
                wins.append(_Window(q_ref[c * QB:(c + 1) * QB, :], kblks, vblks, exists, tt))

        order = _after if (do_attn and do_mlp) else (lambda val, _deps: val)
        if do_attn:
            ns_all = [w.scores() for w in wins]
        if do_mlp:
            yl = order(yl_ref[...], ns_all if do_attn else None)
            y = jnp.concatenate([yl[:, 0:2 * g], yc_scr[...], yl[:, 2 * g:3 * g]], axis=1)
            x1 = x_ref[...] + jnp.dot(y, wo_ref[...], preferred_element_type=F32)
        if do_attn:
            ll_all = [w.suffix_sums() for w in wins]
        if do_mlp:
            h = order(_rms(x1) * gf_ref[...], ll_all if do_attn else None).astype(BF16)
            a = jnp.maximum(jnp.dot(h, wup_ref[...], preferred_element_type=F32), 0.0)
        if do_attn:
            carries = [w.weights() for w in wins]
            for c in range(NQ):
                carry_scr[c] = carries[c]
            live_any = jnp.max(functools.reduce(jnp.maximum, carries))
            pv_all = [w.values() for w in wins]
            for c in range(NQ):
                acc_scr[c] = pv_all[c]
                yc_scr[c * QB:(c + 1) * QB, :] = (_rms(pv_all[c]) * gc_ref[...]).astype(BF16)
        if do_mlp:
            a2 = order(a * a, pv_all if do_attn else None).astype(BF16)
            x2 = x1 + jnp.dot(a2, wdn_ref[...], preferred_element_type=F32)
            if final_norm:
                x2 = _rms(x2) * gfin_ref[...]
            o_ref[...] = x2
        if not do_attn:
            return

        def fetch(src, dst, slot, kb):
            cp = pltpu.make_async_copy(src.at[bi, pl.ds(pl.multiple_of(kb * QB, QB), QB), :], dst, sem.at[slot])
            cp.start()
            return cp

        def older_blocks(c, _):
            qs = _stack_heads(q_ref[pl.ds(pl.multiple_of(c * QB, QB), QB), :])

            def cond(st):
                kb, live = st
                return jnp.logical_and(kb >= 0, live > -STICK_DEAD)

            def body(st):
                kb, _ = st
                ck = fetch(k_hbm, kbuf, 0, kb)
                cv = fetch(v_hbm, vbuf, 1, kb)
                ck.wait()
                cv.wait()
                carry = carry_scr[c]
                pv, tot = _attn_block(qs, kbuf[...], vbuf[...], tt, carry)
                acc_scr[c] = acc_scr[c] + pv
                carry_scr[c] = carry + tot
                return kb - 1, jnp.max(carry + tot)

            lax.while_loop(cond, body, (i * NQ + c - ATTN_WINDOW, jnp.max(carry_scr[c])))
            return 0

        @pl.when(live_any > -STICK_DEAD)
        def _():
            lax.fori_loop(0, NQ, older_blocks, 0)
            for c in range(NQ):
                yc_scr[c * QB:(c + 1) * QB, :] = (_rms(acc_scr[c]) * gc_ref[...]).astype(BF16)

    @pl.when(i == 0)
    def _():
        step(True, False)

    @pl.when(jnp.logical_and(i > 0, i < n_tiles))
    def _():
        step(True, True)

    @pl.when(i == n_tiles)
    def _():
        step(False, True)


def _attn_ffn(l, x, y_loc, q, k, v, w_out, g_ffn, w_up, w_dn, g_fin, g_att, final_norm):
    b, s, d = x.shape
    g = D_GROUP
    nt = s // TM
    lag = lambda bi, i: (bi, jnp.maximum(i - 1, 0), 0)
    cur = lambda bi, i: (bi, jnp.minimum(i, nt - 1), 0)
    prev = lambda bi, i: (bi, jnp.maximum(jnp.minimum(i, nt - 1) - 1, 0), 0)
    const = lambda bi, i: (0, 0)
    resident = lambda w: pl.BlockSpec(memory_space=pl.ANY)
    return pl.pallas_call(
        functools.partial(_attn_ffn_kernel, layer=l, n_tiles=nt, final_norm=final_norm),
        grid=(b, nt + 1),
        in_specs=[
            pl.BlockSpec((None, TM, d), lag),
            pl.BlockSpec((None, TM, 3 * g), lag),
            pl.BlockSpec((None, TM, g), cur),
            pl.BlockSpec((None, TM, g), cur),
            pl.BlockSpec((None, TM, g), cur),
            pl.BlockSpec((None, TM, g), prev),
            pl.BlockSpec((None, TM, g), prev),
            pl.BlockSpec(memory_space=pl.ANY),
            pl.BlockSpec(memory_space=pl.ANY),
            resident(w_out),
            pl.BlockSpec((1, d), const),
            resident(w_up),
            resident(w_dn),
            pl.BlockSpec((1, d), const),
            pl.BlockSpec((1, g), const),
        ],
        out_specs=pl.BlockSpec((None, TM, d), lag),
        out_shape=jax.ShapeDtypeStruct((b, s, d), F32),
        scratch_shapes=[
            pltpu.VMEM((TM, g), BF16),
            pltpu.VMEM((NQ, N_HEADS * QB, QB), F32),
            pltpu.VMEM((NQ, QB, g), F32),
            pltpu.VMEM((QB, g), BF16),
            pltpu.VMEM((QB, g), BF16),
            pltpu.SemaphoreType.DMA((2,)),
            pltpu.VMEM(w_out.shape[1:], BF16),
            pltpu.VMEM(w_up.shape[1:], BF16),
            pltpu.VMEM(w_dn.shape[1:], BF16),
            pltpu.VMEM((2, WEIGHT_STAGE_ROWS, d), F32),
            pltpu.VMEM((2, WEIGHT_STAGE_ROWS * d // w_up.shape[2], w_up.shape[2]), F32),
            pltpu.SemaphoreType.DMA((4,)),
        ],
        compiler_params=pltpu.CompilerParams(
            dimension_semantics=("arbitrary", "arbitrary"), vmem_limit_bytes=56 * MIB),
        name="attn_ffn",
    )(x, y_loc, q, k, v, k, v, k, v, w_out, g_ffn, w_up, w_dn, g_fin, g_att)


def kernel(x, norm_mix_g, w_in, gmlp_v_g, gmlp_w_s, gmlp_b_s, short_conv_w, conf_conv_w, conf_ln_g, conf_ln_b,
           mix_out_g, w_out, norm_ffn_g, w_up, w_down, final_norm_g):
    depth = w_in.shape[0]
    g = D_GROUP
    for l in range(depth):
        ws_cat = jnp.transpose(gmlp_w_s[l], (1, 0, 2)).reshape(CHUNK, N_HEADS * CHUNK)
        bs_lanes = jnp.repeat(gmlp_b_s[l].T, HEAD_DIM, axis=1)
        go = mix_out_g[l]
        g_loc = jnp.concatenate([go[0:2 * g], go[3 * g:4 * g]])[None, :]
        y_loc, q, k, v = _proj_local(
            l, x, norm_mix_g[l][None, :], w_in, gmlp_v_g[l][None, :], ws_cat, bs_lanes,
            short_conv_w[l], conf_conv_w[l], conf_ln_g[l][None, :], conf_ln_b[l][None, :], g_loc)
        x = _attn_ffn(
            l, x, y_loc, q, k, v, w_out, norm_ffn_g[l][None, :], w_up, w_down,
            final_norm_g[None, :], go[2 * g:3 * g][None, :], final_norm=(l == depth - 1))
    return x
```
